```python
import jax
import jax.numpy as jnp
from jax import lax
import numpy as np

D_MODEL = 2048
BATCH = 8
SEQ = 4096
DEPTH = 1

M_HEADS = 4
M_QK = 128
M_V = 256
CONV_W = 4
G_HEADS = 4
G_K = 128
G_V = 256
G_RANK = 16
G_TAU = 16.0
CHUNK = 64
N_EXPERTS = 32
TOP_K = 4
D_FF = D_MODEL
SWIGLU_ALPHA = 1.702
SWIGLU_LIMIT = 7.0
MOE_BLOCK = 256
NORM_EPS = 1e-5
DN_ALPHA = (2 * DEPTH) ** 0.25
DN_BETA = (8 * DEPTH) ** -0.25

M_QKW = M_HEADS * M_QK
M_VW = M_HEADS * M_V
G_KW = G_HEADS * G_K
G_VW = G_HEADS * G_V
SPLIT_SIZES = (M_QKW, M_QKW, M_VW, M_VW, M_HEADS, M_HEADS, G_KW, G_KW, G_VW, G_VW, G_RANK, D_MODEL, D_MODEL)
D_IN = sum(SPLIT_SIZES)
F_GATE_COL = sum(SPLIT_SIZES[:5])

kernel_name = 'hybrid_mlstm_gla_moe_deepnorm'


def _layer_norm(x, g, b):
    xf = x.astype(jnp.float32)
    mu = jnp.mean(xf, axis=-1, keepdims=True)
    var = jnp.mean(jnp.square(xf - mu), axis=-1, keepdims=True)
    return ((xf - mu) * lax.rsqrt(var + NORM_EPS)).astype(x.dtype) * g + b


def _head_layer_norm(h, w, dtype):
    B, S = h.shape[:2]
    mu = jnp.mean(h, axis=-1, keepdims=True)
    var = jnp.mean(jnp.square(h - mu), axis=-1, keepdims=True)
    return ((h - mu) * lax.rsqrt(var + NORM_EPS)).reshape(B, S, -1).astype(dtype) * w


def _head_rms_norm(h, w, dtype):
    B, S = h.shape[:2]
    ms = jnp.mean(jnp.square(h), axis=-1, keepdims=True)
    return (h * lax.rsqrt(ms + NORM_EPS)).reshape(B, S, -1).astype(dtype) * w


def _causal_dwconv(u, w, b):
    C = u.shape[-1]
    y = lax.conv_general_dilated(u, w[:, None, :].astype(u.dtype), window_strides=(1,),
                                 padding=[(CONV_W - 1, 0)], dimension_numbers=('NWC', 'WIO', 'NWC'),
                                 feature_group_count=C)
    return y + b


def _to_chunks(t):
    B, S = t.shape[:2]
    t = t.reshape((B, S // CHUNK, CHUNK) + t.shape[2:])
    return jnp.moveaxis(t, 3, 1)


def _from_chunks(t):
    B, H, NC, L, d = t.shape
    return jnp.moveaxis(t, 1, 3).reshape(B, NC * L, H, d)


def _mlstm(q, k, v, i_pre, f_pre):
    B, S, H, DK = q.shape
    DV = v.shape[-1]
    f32 = jnp.float32
    q = _to_chunks(q).astype(f32)
    k = _to_chunks(k).astype(f32) * DK ** -0.5
    v = _to_chunks(v).astype(f32)
    log_i = _to_chunks(i_pre).astype(f32)
    log_f = jax.nn.log_sigmoid(_to_chunks(f_pre).astype(f32))
    b = jnp.cumsum(log_f, axis=-1)
    b_end = b[..., -1]
    causal = jnp.tril(jnp.ones((CHUNK, CHUNK), dtype=bool))
    d_log = jnp.where(causal, b[..., :, None] - b[..., None, :] + log_i[..., None, :], -jnp.inf)
    w_end = b_end[..., None] - b + log_i
    g_end = jnp.max(w_end, axis=-1)
    ke = k * jnp.exp(w_end - g_end[..., None])[..., None]
    dC = jnp.einsum('bhcsd,bhcse->bhcde', ke, v)
    dn = jnp.sum(ke, axis=-2)

    def step(carry, xs):
        C, n, m = carry
        dC_c, dn_c, g_c, be_c = xs
        m_new = jnp.maximum(be_c + m, g_c)
        a = jnp.exp(be_c + m - m_new)
        c = jnp.exp(g_c - m_new)
        C_new = a[..., None, None] * C + c[..., None, None] * dC_c
        n_new = a[..., None] * n + c[..., None] * dn_c
        return (C_new, n_new, m_new), (C, n, m)

    init = (jnp.zeros((B, H, DK, DV), f32), jnp.zeros((B, H, DK), f32), jnp.zeros((B, H), f32))
    xs = (jnp.moveaxis(dC, 2, 0), jnp.moveaxis(dn, 2, 0), jnp.moveaxis(g_end, 2, 0), jnp.moveaxis(b_end, 2, 0))
    _, (C_prev, n_prev, m_prev) = lax.scan(step, init, xs)
    C_prev = jnp.moveaxis(C_prev, 0, 2)
    n_prev = jnp.moveaxis(n_prev, 0, 2)
    m_prev = jnp.moveaxis(m_prev, 0, 2)
    m_inter = b + m_prev[..., None]
    m_t = jnp.maximum(m_inter, jnp.max(d_log, axis=-1))
    dec = jnp.exp(m_inter - m_t)
    s = jnp.einsum('bhcld,bhcsd->bhcls', q, k) * jnp.exp(d_log - m_t[..., None])
    num = jnp.einsum('bhcls,bhcse->bhcle', s, v) + dec[..., None] * jnp.einsum('bhcld,bhcde->bhcle', q, C_prev)
    den = jnp.sum(s, axis=-1) + dec * jnp.einsum('bhcld,bhcd->bhcl', q, n_prev)
    h = num / jnp.maximum(jnp.abs(den), jnp.exp(-m_t))[..., None]
    return _from_chunks(h)


def _gla(q, k, v, log_a):
    B, S, H, DK = q.shape
    DV = v.shape[-1]
    f32 = jnp.float32
    q = _to_chunks(q).astype(f32) * DK ** -0.5
    k = _to_chunks(k).astype(f32)
    v = _to_chunks(v).astype(f32)
    cum = jnp.cumsum(_to_chunks(log_a).astype(f32), axis=3)
    cum_end = cum[..., -1, :]
    q_in = q * jnp.exp(cum)
    k_in = k * jnp.exp(-cum)
    k_end = k * jnp.exp(cum_end[..., None, :] - cum)
    dS = jnp.einsum('bhcsd,bhcse->bhcde', k_end, v)

    def step(S_c, xs):
        dec_c, dS_c = xs
        return dec_c[..., None] * S_c + dS_c, S_c

    _, S_prev = lax.scan(step, jnp.zeros((B, H, DK, DV), f32),
                         (jnp.moveaxis(jnp.exp(cum_end), 2, 0), jnp.moveaxis(dS, 2, 0)))
    S_prev = jnp.moveaxis(S_prev, 0, 2)
    causal = jnp.tril(jnp.ones((CHUNK, CHUNK), dtype=bool))
    s = jnp.where(causal, jnp.einsum('bhcld,bhcsd->bhcls', q_in, k_in), 0.0)
    o = jnp.einsum('bhcls,bhcse->bhcle', s, v) + jnp.einsum('bhcld,bhcde->bhcle', q_in, S_prev)
    return _from_chunks(o)


def _token_mix(u, w_in, b_in, conv_w, conv_b, w_gla_gate, b_gla_gate, m_norm_w, g_norm_w,
               w_branch_m, w_branch_g, w_out):
    B, S, _ = u.shape
    proj = u @ w_in + b_in
    split_idx = [int(c) for c in np.cumsum(SPLIT_SIZES)[:-1]]
    mq, mk, mv, mo, mi, mf, gq, gk, gv, gg, glr, gate_m, gate_g = jnp.split(proj, split_idx, axis=-1)
    qk = jax.nn.silu(_causal_dwconv(jnp.concatenate([mq, mk], axis=-1), conv_w, conv_b))
    q_m = qk[..., :M_QKW].reshape(B, S, M_HEADS, M_QK)
    k_m = qk[..., M_QKW:].reshape(B, S, M_HEADS, M_QK)
    h_m = _mlstm(q_m, k_m, mv.reshape(B, S, M_HEADS, M_V), mi, mf)
    y_m = _head_layer_norm(h_m, m_norm_w, u.dtype) * jax.nn.sigmoid(mo)
    log_a = jax.nn.log_sigmoid((glr @ w_gla_gate + b_gla_gate).astype(jnp.float32)) / G_TAU
    o_g = _gla(gq.reshape(B, S, G_HEADS, G_K), gk.reshape(B, S, G_HEADS, G_K),
               gv.reshape(B, S, G_HEADS, G_V), log_a.reshape(B, S, G_HEADS, G_K))
    y_g = _head_rms_norm(o_g, g_norm_w, u.dtype) * jax.nn.silu(gg)
    z = jax.nn.sigmoid(gate_m) * (y_m @ w_branch_m) + jax.nn.sigmoid(gate_g) * (y_g @ w_branch_g)
    return z @ w_out


def _moe(h, w_router, b_router, w_gu, b_gu, w_down, b_down):
    B, S, D = h.shape
    T = B * S
    A = T * TOP_K
    hf = h.reshape(T, D)
    logits = (hf @ w_router + b_router).astype(jnp.float32)
    top_v, top_e = lax.top_k(logits, TOP_K)
    gates = jax.nn.softmax(top_v, axis=-1).astype(h.dtype)
    e_flat = top_e.reshape(A).astype(jnp.int32)
    tok_flat = jnp.arange(A, dtype=jnp.int32) // TOP_K
    g_flat = gates.reshape(A)
    order = jnp.argsort(e_flat)
    e_sorted = e_flat[order]
    counts = jnp.zeros((N_EXPERTS,), jnp.int32).at[e_flat].add(1)
    padded = (counts + MOE_BLOCK - 1) // MOE_BLOCK * MOE_BLOCK
    start = jnp.cumsum(counts) - counts
    pend = jnp.cumsum(padded)
    pstart = pend - padded
    dest = pstart[e_sorted] + (jnp.arange(A, dtype=jnp.int32) - start[e_sorted])
    n_blocks = -(-A // MOE_BLOCK) + N_EXPERTS
    P = n_blocks * MOE_BLOCK
    buf_tok = jnp.full((P,), T, jnp.int32).at[dest].set(tok_flat[order])
    buf_gate = jnp.zeros((P,), h.dtype).at[dest].set(g_flat[order])
    block_e = jnp.searchsorted(pend, jnp.arange(n_blocks, dtype=jnp.int32) * MOE_BLOCK, side='right')
    block_e = jnp.minimum(block_e, N_EXPERTS - 1).astype(jnp.int32)
    h_pad = jnp.concatenate([hf, jnp.zeros((1, D), hf.dtype)], axis=0)

    def run_block(args):
        tok, gate, e = args
        xb = h_pad[tok]
        gu = xb @ w_gu[e] + b_gu[e]
        a_glu = jnp.minimum(gu[:, :D_FF], SWIGLU_LIMIT)
        a_lin = jnp.clip(gu[:, D_FF:], -SWIGLU_LIMIT, SWIGLU_LIMIT)
        act = a_glu * jax.nn.sigmoid(SWIGLU_ALPHA * a_glu) * (a_lin + 1.0)
        return (act @ w_down[e] + b_down[e]) * gate[:, None]

    out = lax.map(run_block, (buf_tok.reshape(n_blocks, MOE_BLOCK), buf_gate.reshape(n_blocks, MOE_BLOCK), block_e))
    y = jax.ops.segment_sum(out.reshape(P, D), buf_tok, num_segments=T + 1)[:T]
    return y.reshape(B, S, D)


def setup_inputs(seed: int = 0) -> dict:
    key = jax.random.key(seed)
    ks = jax.random.split(key, 22)
    L = DEPTH
    f32 = jnp.float32

    def nrm(k, shape, scale):
        return jax.random.normal(k, shape, f32) * scale

    f_bias = jnp.linspace(3.0, 6.0, M_HEADS, dtype=f32)
    return {
        'x': nrm(ks[0], (BATCH, SEQ, D_MODEL), 1.0),
        'w_in': nrm(ks[1], (L, D_MODEL, D_IN), D_MODEL ** -0.5),
        'b_in': nrm(ks[2], (L, D_IN), 0.02).at[:, F_GATE_COL:F_GATE_COL + M_HEADS].add(f_bias),
        'conv_w': nrm(ks[3], (L, CONV_W, 2 * M_QKW), CONV_W ** -0.5),
        'conv_b': nrm(ks[4], (L, 2 * M_QKW), 0.02),
        'w_gla_gate': nrm(ks[5], (L, G_RANK, G_KW), G_RANK ** -0.5),
        'b_gla_gate': nrm(ks[6], (L, G_KW), 0.02),
        'm_norm_w': 1.0 + nrm(ks[7], (L, M_VW), 0.02),
        'g_norm_w': 1.0 + nrm(ks[8], (L, G_VW), 0.02),
        'w_branch_m': nrm(ks[9], (L, M_VW, D_MODEL), M_VW ** -0.5),
        'w_branch_g': nrm(ks[10], (L, G_VW, D_MODEL), G_VW ** -0.5),
        'w_out': nrm(ks[11], (L, D_MODEL, D_MODEL), DN_BETA * D_MODEL ** -0.5),
        'ln1_g': 1.0 + nrm(ks[12], (L, D_MODEL), 0.02),
        'ln1_b': nrm(ks[13], (L, D_MODEL), 0.02),
        'w_router': nrm(ks[14], (L, D_MODEL, N_EXPERTS), D_MODEL ** -0.5),
        'b_router': nrm(ks[15], (L, N_EXPERTS), 0.01),
        'w_gu': nrm(ks[16], (L, N_EXPERTS, D_MODEL, 2 * D_FF), D_MODEL ** -0.5),
        'b_gu': nrm(ks[17], (L, N_EXPERTS, 2 * D_FF), 0.02),
        'w_down': nrm(ks[18], (L, N_EXPERTS, D_FF, D_MODEL), DN_BETA * D_FF ** -0.5),
        'b_down': nrm(ks[19], (L, N_EXPERTS, D_MODEL), 0.02),
        'ln2_g': 1.0 + nrm(ks[20], (L, D_MODEL), 0.02),
        'ln2_b': nrm(ks[21], (L, D_MODEL), 0.02),
    }


def reference(x, w_in, b_in, conv_w, conv_b, w_gla_gate, b_gla_gate, m_norm_w, g_norm_w,
              w_branch_m, w_branch_g, w_out, ln1_g, ln1_b, w_router, b_router, w_gu, b_gu,
              w_down, b_down, ln2_g, ln2_b):
    for l in range(DEPTH):
        mix = _token_mix(x, w_in[l], b_in[l], conv_w[l], conv_b[l], w_gla_gate[l], b_gla_gate[l],
                         m_norm_w[l], g_norm_w[l], w_branch_m[l], w_branch_g[l], w_out[l])
        x = _layer_norm(DN_ALPHA * x + mix, ln1_g[l], ln1_b[l])
        ff = _moe(x, w_router[l], b_router[l], w_gu[l], b_gu[l], w_down[l], b_down[l])
        x = _layer_norm(DN_ALPHA * x + ff, ln2_g[l], ln2_b[l])
    return x
```

```python
import functools

import jax
import jax.numpy as jnp
from jax import lax
from jax.experimental import pallas as pl
from jax.experimental.pallas import tpu as pltpu

F32 = jnp.float32
BF16 = jnp.bfloat16

M_HEADS, M_QK, M_V, CONV_W = 4, 128, 256, 4
G_HEADS, G_K, G_V, G_RANK, G_TAU = 4, 128, 256, 16, 16.0
N_EXPERTS, TOP_K = 32, 4
SWIGLU_ALPHA, SWIGLU_LIMIT = 1.702, 7.0
NORM_EPS = 1e-5
M_QKW, M_VW = M_HEADS * M_QK, M_HEADS * M_V
G_KW, G_VW = G_HEADS * G_K, G_HEADS * G_V

LANES = 128
SUBLANES = 8
VMEM_LIMIT = 56 * 1024 * 1024

PROJ_TM, PROJ_TN = 1024, 1024
SMALL_W = LANES
M_CHUNK = 256
G_STEP = 256
G_CHUNK = 64
MIX_TM = 256
EXP_TM = 512
EXP_TF = 512
DISP_TM = 512
COMB_TM = 256


def _dot(a, b):
    return jnp.dot(a, b, preferred_element_type=F32)


def _dot_nt(a, b):
    return lax.dot_general(a, b, (((1,), (1,)), ((), ())), preferred_element_type=F32)


def _dot_tn(a, b):
    return lax.dot_general(a, b, (((0,), (0,)), ((), ())), preferred_element_type=F32)


def _split3(x):
    hi = x.astype(BF16)
    r = x - hi.astype(F32)
    mid = r.astype(BF16)
    lo = (r - mid.astype(F32)).astype(BF16)
    return hi, mid, lo


def _sigmoid(x):
    return 1.0 / (1.0 + jnp.exp(-x))


def _log_sigmoid(x):
    return jnp.minimum(x, 0.0) - jnp.log1p(jnp.exp(-jnp.abs(x)))


def _params(sem):
    return pltpu.CompilerParams(dimension_semantics=sem, vmem_limit_bytes=VMEM_LIMIT)


def _proj_kernel(x_ref, w_ref, b_ref, o_ref):
    o_ref[...] = _dot(x_ref[...], w_ref[...]) + b_ref[...]


def _proj(x_bf, w_bf, b, tn):
    T, K = x_bf.shape
    N = w_bf.shape[1]
    tm = min(PROJ_TM, T)
    return pl.pallas_call(
        _proj_kernel,
        grid=(N // tn, T // tm),
        in_specs=[pl.BlockSpec((tm, K), lambda j, i: (i, 0)),
                  pl.BlockSpec((K, tn), lambda j, i: (0, j)),
                  pl.BlockSpec((1, tn), lambda j, i: (0, j))],
        out_specs=pl.BlockSpec((tm, tn), lambda j, i: (i, j)),
        out_shape=jax.ShapeDtypeStruct((T, N), F32),
        compiler_params=_params(("parallel", "parallel")),
    )(x_bf, w_bf, b)


def _mlstm_kernel(qk_ref, v_ref, og_ref, sm_ref, smt_ref, cw_ref, cb_ref, nw_ref, o_ref,
                  ext_ref, c_ref, n_ref, m_ref):
    L = M_CHUNK
    c = pl.program_id(1)

    @pl.when(c == 0)
    def _():
        ext_ref[0:SUBLANES, :] = jnp.zeros((SUBLANES, 2 * M_QKW), F32)
        c_ref[...] = jnp.zeros_like(c_ref)
        n_ref[...] = jnp.zeros_like(n_ref)
        m_ref[...] = jnp.zeros_like(m_ref)

    ext_ref[SUBLANES:, :] = qk_ref[...]
    acc = cb_ref[...] + cw_ref[CONV_W - 1:CONV_W, :] * ext_ref[SUBLANES:, :]
    for j in range(CONV_W - 1):
        off = SUBLANES - (CONV_W - 1) + j
        acc = acc + cw_ref[j:j + 1, :] * ext_ref[off:off + L, :]
    ext_ref[0:SUBLANES, :] = ext_ref[L:L + SUBLANES, :]
    qk = acc * _sigmoid(acc)

    row = lax.broadcasted_iota(jnp.int32, (L, L), 0)
    col = lax.broadcasted_iota(jnp.int32, (L, L), 1)
    causal = row >= col
    tri = jnp.where(causal, 1.0, 0.0).astype(BF16)
    triu = jnp.where(col >= row, 1.0, 0.0).astype(BF16)

    sm = sm_ref[...]
    smt = smt_ref[...]
    h1, h2, h3 = _split3(_log_sigmoid(sm))
    bcol_all = _dot(tri, h1) + _dot(tri, h2) + _dot(tri, h3)
    r1, r2, r3 = _split3(_log_sigmoid(smt))
    brow_all = _dot(r1, triu) + _dot(r2, triu) + _dot(r3, triu)

    for h in range(M_HEADS):
        q = qk[:, h * M_QK:(h + 1) * M_QK]
        k = qk[:, M_QKW + h * M_QK:M_QKW + (h + 1) * M_QK] * (M_QK ** -0.5)
        v = v_ref[:, h * M_V:(h + 1) * M_V]
        qb, kb, vb = q.astype(BF16), k.astype(BF16), v.astype(BF16)
        li_col = sm[:, h:h + 1]
        b_col = bcol_all[:, M_HEADS + h:M_HEADS + h + 1]
        b_end = b_col[L - 1:L, :]
        li_row = smt[h:h + 1, :]
        b_row = brow_all[M_HEADS + h:M_HEADS + h + 1, :]
        m_prev = m_ref[h:h + 1, 0:1]
        n_prev = n_ref[h:h + 1, :]
        c_prev = c_ref[h]

        d_log = jnp.where(causal, b_col + (li_row - b_row), -jnp.inf)
        m_inter = b_col + m_prev
        m_t = jnp.maximum(m_inter, jnp.max(d_log, axis=-1, keepdims=True))
        dec = jnp.exp(m_inter - m_t)
        s = _dot_nt(qb, kb) * jnp.exp(d_log - m_t)
        num = _dot(s.astype(BF16), vb) + dec * _dot(qb, c_prev.astype(BF16))
        den = jnp.sum(s, axis=-1, keepdims=True) + dec * jnp.sum(q * n_prev, axis=-1, keepdims=True)
        hh = num / jnp.maximum(jnp.abs(den), jnp.exp(-m_t))

        w_end = b_end - b_col + li_col
        g_end = jnp.max(w_end, axis=0, keepdims=True)
        ke = k * jnp.exp(w_end - g_end)
        d_c = _dot_tn(ke.astype(BF16), vb)
        d_n = jnp.sum(ke, axis=0, keepdims=True)
        m_new = jnp.maximum(b_end + m_prev, g_end)
        a = jnp.exp(b_end + m_prev - m_new)
        cc = jnp.exp(g_end - m_new)
        c_ref[h] = a * c_prev + cc * d_c
        n_ref[h:h + 1, :] = a * n_prev + cc * d_n
        m_ref[h:h + 1, :] = jnp.broadcast_to(m_new, (1, LANES))

        mu = jnp.mean(hh, axis=-1, keepdims=True)
        xc = hh - mu
        var = jnp.mean(xc * xc, axis=-1, keepdims=True)
        sl = slice(h * M_V, (h + 1) * M_V)
        y = xc * lax.rsqrt(var + NORM_EPS) * nw_ref[:, sl] * _sigmoid(og_ref[:, sl])
        o_ref[:, sl] = y.astype(o_ref.dtype)


def _mlstm(proj, small, small_t, conv_w, conv_b, norm_w, B, S):
    L = M_CHUNK
    NC = S // L
    T = B * S
    rowblk = lambda b, c: (b * NC + c, 0)
    const = lambda b, c: (0, 0)
    return pl.pallas_call(
        _mlstm_kernel,
        grid=(B, NC),
        in_specs=[pl.BlockSpec((L, 2 * M_QKW), rowblk),
                  pl.BlockSpec((L, M_VW), lambda b, c: (b * NC + c, 1)),
                  pl.BlockSpec((L, M_VW), lambda b, c: (b * NC + c, 2)),
                  pl.BlockSpec((L, SMALL_W), rowblk),
                  pl.BlockSpec((SUBLANES, L), lambda b, c: (0, b * NC + c)),
                  pl.BlockSpec((CONV_W, 2 * M_QKW), const),
                  pl.BlockSpec((1, 2 * M_QKW), const),
                  pl.BlockSpec((1, M_VW), const)],
        out_specs=pl.BlockSpec((L, M_VW), rowblk),
        out_shape=jax.ShapeDtypeStruct((T, M_VW), BF16),
        scratch_shapes=[pltpu.VMEM((L + SUBLANES, 2 * M_QKW), F32),
                        pltpu.VMEM((M_HEADS, M_QK, M_V), F32),
                        pltpu.VMEM((SUBLANES, M_QK), F32),
                        pltpu.VMEM((SUBLANES, LANES), F32)],
        compiler_params=_params(("parallel", "arbitrary")),
    )(proj, proj, proj, small, small_t, conv_w, conv_b, norm_w)


def _gla_kernel(qk_ref, v_ref, gg_ref, sm_ref, wg_ref, bg_ref, nw_ref, o_ref, st_ref):
    LC = G_CHUNK
    c = pl.program_id(1)

    @pl.when(c == 0)
    def _():
        st_ref[...] = jnp.zeros_like(st_ref)

    pre = _dot(sm_ref[...].astype(BF16), wg_ref[...]) + bg_ref[...]
    log_a = _log_sigmoid(pre) / G_TAU

    row = lax.broadcasted_iota(jnp.int32, (LC, LC), 0)
    col = lax.broadcasted_iota(jnp.int32, (LC, LC), 1)
    causal = row >= col
    tri = jnp.where(causal, 1.0, 0.0).astype(BF16)

    for j in range(G_STEP // LC):
        rs = slice(j * LC, (j + 1) * LC)
        a1, a2, a3 = _split3(log_a[rs, :])
        cum = _dot(tri, a1) + _dot(tri, a2) + _dot(tri, a3)
        cum_end = cum[LC - 1:LC, :]
        e_q = jnp.exp(cum)
        e_k = jnp.exp(-cum)
        e_end = jnp.exp(cum_end - cum)
        e_dec = jnp.exp(cum_end)
        for h in range(G_HEADS):
            ks = slice(h * G_K, (h + 1) * G_K)
            vs = slice(h * G_V, (h + 1) * G_V)
            q = qk_ref[rs, ks] * (G_K ** -0.5)
            k = qk_ref[rs, G_KW + h * G_K:G_KW + (h + 1) * G_K]
            vb = v_ref[rs, vs].astype(BF16)
            q_in = (q * e_q[:, ks]).astype(BF16)
            k_in = (k * e_k[:, ks]).astype(BF16)
            k_end = (k * e_end[:, ks]).astype(BF16)
            st = st_ref[h]
            s = jnp.where(causal, _dot_nt(q_in, k_in), 0.0)
            o = _dot(s.astype(BF16), vb) + _dot_nt(q_in, st.astype(BF16))
            st_ref[h] = st * e_dec[:, ks] + _dot_tn(vb, k_end)
            ms = jnp.mean(o * o, axis=-1, keepdims=True)
            g = gg_ref[rs, vs]
            y = o * lax.rsqrt(ms + NORM_EPS) * nw_ref[:, vs] * (g * _sigmoid(g))
            o_ref[rs, vs] = y.astype(o_ref.dtype)


def _gla(proj, small, wg_pad, bg, norm_w, B, S):
    L = G_STEP
    NC = S // L
    T = B * S
    rowblk = lambda b, c: (b * NC + c, 0)
    const = lambda b, c: (0, 0)
    return pl.pallas_call(
        _gla_kernel,
        grid=(B, NC),
        in_specs=[pl.BlockSpec((L, 2 * G_KW), lambda b, c: (b * NC + c, 3)),
                  pl.BlockSpec((L, G_VW), lambda b, c: (b * NC + c, 4)),
                  pl.BlockSpec((L, G_VW), lambda b, c: (b * NC + c, 5)),
                  pl.BlockSpec((L, SMALL_W), rowblk),
                  pl.BlockSpec((SMALL_W, G_KW), const),
                  pl.BlockSpec((1, G_KW), const),
                  pl.BlockSpec((1, G_VW), const)],
        out_specs=pl.BlockSpec((L, G_VW), rowblk),
        out_shape=jax.ShapeDtypeStruct((T, G_VW), BF16),
        scratch_shapes=[pltpu.VMEM((G_HEADS, G_V, G_K), F32)],
        compiler_params=_params(("parallel", "arbitrary")),
    )(proj, proj, proj, small, wg_pad, bg, norm_w)


def _layer_norm(r, g, b):
    mu = jnp.mean(r, axis=-1, keepdims=True)
    xc = r - mu
    var = jnp.mean(xc * xc, axis=-1, keepdims=True)
    return xc * lax.rsqrt(var + NORM_EPS) * g + b


def _mix_kernel(alpha, ym_ref, yg_ref, gm_ref, gg_ref, x_ref, wbm_ref, wbg_ref, wout_ref, lg_ref, lb_ref,
                wr_ref, br_ref, x1_ref, tope_ref, gate_ref, rank_ref, cnt_ref, carry_ref):
    tm = x_ref.shape[0]
    i = pl.program_id(0)

    @pl.when(i == 0)
    def _():
        carry_ref[...] = jnp.zeros_like(carry_ref)

    pm = _dot(ym_ref[...], wbm_ref[...])
    pg = _dot(yg_ref[...], wbg_ref[...])
    z = _sigmoid(gm_ref[...]) * pm + _sigmoid(gg_ref[...]) * pg
    mix = _dot(z.astype(BF16), wout_ref[...])
    x1 = _layer_norm(alpha * x_ref[...] + mix, lg_ref[...], lb_ref[...])
    x1_ref[...] = x1

    logits = _dot_nt(wr_ref[...], x1.astype(BF16)) + br_ref[...]
    eidx = lax.broadcasted_iota(jnp.int32, (N_EXPERTS, tm), 0)
    vals, hots = [], []
    cur = logits
    for k in range(TOP_K):
        mx = jnp.max(cur, axis=0, keepdims=True)
        idx = jnp.min(jnp.where(cur == mx, eidx, N_EXPERTS), axis=0, keepdims=True)
        hot = eidx == idx
        cur = jnp.where(hot, -jnp.inf, cur)
        vals.append(mx)
        hots.append(hot)
        tope_ref[k:k + 1, :] = idx
    exps = [jnp.exp(v - vals[0]) for v in vals]
    tot = exps[0] + exps[1] + exps[2] + exps[3]
    for k in range(TOP_K):
        gate_ref[k:k + 1, :] = exps[k] / tot

    sel = jnp.zeros((N_EXPERTS, tm), F32)
    for hot in hots:
        sel = sel + jnp.where(hot, 1.0, 0.0)
    row = lax.broadcasted_iota(jnp.int32, (tm, tm), 0)
    col = lax.broadcasted_iota(jnp.int32, (tm, tm), 1)
    before = jnp.where(row < col, 1.0, 0.0).astype(BF16)
    pos = _dot(sel.astype(BF16), before) + carry_ref[:, 0:1]
    for k in range(TOP_K):
        rk = jnp.sum(jnp.where(hots[k], pos, 0.0), axis=0, keepdims=True)
        rank_ref[k:k + 1, :] = rk.astype(jnp.int32)
    carry_ref[...] = carry_ref[...] + jnp.sum(sel, axis=1, keepdims=True)
    cnt_ref[...] = carry_ref[...]


def _mix(ym, yg, proj, x2d, wbm, wbg, wout, ln_g, ln_b, wr_t, br, alpha):
    T, D = x2d.shape
    tm = min(MIX_TM, T)
    rowblk = lambda i: (i, 0)
    const = lambda i: (0, 0)
    once = dict(pipeline_mode=pl.Buffered(1))
    return pl.pallas_call(
        functools.partial(_mix_kernel, alpha),
        grid=(T // tm,),
        in_specs=[pl.BlockSpec((tm, M_VW), rowblk),
                  pl.BlockSpec((tm, G_VW), rowblk),
                  pl.BlockSpec((tm, D), lambda i: (i, 3)),
                  pl.BlockSpec((tm, D), lambda i: (i, 4)),
                  pl.BlockSpec((tm, D), rowblk),
                  pl.BlockSpec((M_VW, D), const, **once),
                  pl.BlockSpec((G_VW, D), const, **once),
                  pl.BlockSpec((D, D), const, **once),
                  pl.BlockSpec((1, D), const),
                  pl.BlockSpec((1, D), const),
                  pl.BlockSpec((N_EXPERTS, D), const),
                  pl.BlockSpec((N_EXPERTS, 1), const)],
        out_specs=[pl.BlockSpec((tm, D), rowblk),
                   pl.BlockSpec((TOP_K, tm), lambda i: (0, i)),
                   pl.BlockSpec((TOP_K, tm), lambda i: (0, i)),
                   pl.BlockSpec((TOP_K, tm), lambda i: (0, i)),
                   pl.BlockSpec((N_EXPERTS, LANES), const)],
        out_shape=[jax.ShapeDtypeStruct((T, D), F32),
                   jax.ShapeDtypeStruct((TOP_K, T), jnp.int32),
                   jax.ShapeDtypeStruct((TOP_K, T), F32),
                   jax.ShapeDtypeStruct((TOP_K, T), jnp.int32),
                   jax.ShapeDtypeStruct((N_EXPERTS, LANES), F32)],
        scratch_shapes=[pltpu.VMEM((N_EXPERTS, LANES), F32)],
        compiler_params=_params(("arbitrary",)),
    )(ym, yg, proj, proj, x2d, wbm, wbg, wout, ln_g, ln_b, wr_t, br)


def _row_copy(src_hbm, src_row, dst, dst_row, sem):
    return pltpu.make_async_copy(src_hbm.at[pl.ds(src_row, 1)], dst.at[pl.ds(dst_row, 1)], sem)


def _dispatch_kernel(n_tok, tm, dest_ref, x1_hbm, xs_in_hbm, xs_hbm, sem):
    del xs_in_hbm
    base = pl.program_id(0) * tm

    def issue(t, carry):
        for k in range(TOP_K):
            _row_copy(x1_hbm, base + t, xs_hbm, dest_ref[k * n_tok + base + t], sem).start()
        return carry

    lax.fori_loop(0, tm, issue, 0)

    def drain(t, carry):
        for k in range(TOP_K):
            _row_copy(x1_hbm, 0, xs_hbm, 0, sem).wait()
        return carry

    lax.fori_loop(0, tm, drain, 0)


def _dispatch(dest_flat, x1, n_rows):
    T, D = x1.shape
    tm = min(DISP_TM, T)
    zeros = jnp.zeros((n_rows, D), x1.dtype)
    return pl.pallas_call(
        functools.partial(_dispatch_kernel, T, tm),
        grid_spec=pltpu.PrefetchScalarGridSpec(
            num_scalar_prefetch=1,
            grid=(T // tm,),
            in_specs=[pl.BlockSpec(memory_space=pl.ANY), pl.BlockSpec(memory_space=pl.ANY)],
            out_specs=pl.BlockSpec(memory_space=pl.ANY),
            scratch_shapes=[pltpu.SemaphoreType.DMA(())]),
        out_shape=jax.ShapeDtypeStruct((n_rows, D), x1.dtype),
        input_output_aliases={2: 0},
        compiler_params=pltpu.CompilerParams(dimension_semantics=("arbitrary",), has_side_effects=True),
    )(dest_flat, x1, zeros)


def _expert_kernel(te_ref, na_ref, xs_ref, wg_ref, wl_ref, bg_ref, bl_ref, wd_ref, bd_ref, o_ref):
    del te_ref
    i = pl.program_id(0)
    f = pl.program_id(1)

    @pl.when(i < na_ref[0])
    def _():
        a = xs_ref[...].astype(BF16)
        g = _dot(a, wg_ref[...]) + bg_ref[...]
        l = _dot(a, wl_ref[...]) + bl_ref[...]
        a_glu = jnp.minimum(g, SWIGLU_LIMIT)
        a_lin = jnp.clip(l, -SWIGLU_LIMIT, SWIGLU_LIMIT)
        act = a_glu * _sigmoid(SWIGLU_ALPHA * a_glu) * (a_lin + 1.0)
        part = _dot(act.astype(BF16), wd_ref[...])

        @pl.when(f == 0)
        def _():
            o_ref[...] = part + bd_ref[...]

        @pl.when(f > 0)
        def _():
            o_ref[...] += part


def _experts(tile_e, n_active, xs, w_gu, b_gu, w_down, b_down):
    P, D = xs.shape
    E, _, F2 = w_gu.shape
    Fh = F2 // 2
    tf = min(EXP_TF, Fh)
    nf = Fh // tf
    n_tiles = P // EXP_TM

    def tile(i, na):
        return jnp.minimum(i, na[0] - 1)

    def fsel(i, f, na):
        return jnp.where(i < na[0], f, nf - 1)

    return pl.pallas_call(
        _expert_kernel,
        grid_spec=pltpu.PrefetchScalarGridSpec(
            num_scalar_prefetch=2,
            grid=(n_tiles, nf),
            in_specs=[
                pl.BlockSpec((EXP_TM, D), lambda i, f, te, na: (tile(i, na), 0)),
                pl.BlockSpec((None, D, tf), lambda i, f, te, na: (te[tile(i, na)], 0, fsel(i, f, na))),
                pl.BlockSpec((None, D, tf), lambda i, f, te, na: (te[tile(i, na)], 0, nf + fsel(i, f, na))),
                pl.BlockSpec((None, 1, tf), lambda i, f, te, na: (te[tile(i, na)], 0, fsel(i, f, na))),
                pl.BlockSpec((None, 1, tf), lambda i, f, te, na: (te[tile(i, na)], 0, nf + fsel(i, f, na))),
                pl.BlockSpec((None, tf, D), lambda i, f, te, na: (te[tile(i, na)], fsel(i, f, na), 0)),
                pl.BlockSpec((None, 1, D), lambda i, f, te, na: (te[tile(i, na)], 0, 0)),
            ],
            out_specs=pl.BlockSpec((EXP_TM, D), lambda i, f, te, na: (tile(i, na), 0))),
        out_shape=jax.ShapeDtypeStruct((P, D), F32),
        compiler_params=_params(("arbitrary", "arbitrary")),
    )(tile_e, n_active, xs, w_gu, w_gu, b_gu, b_gu, w_down, b_down)


def _combine_kernel(alpha, n_tok, dest_ref, ys_hbm, x1_ref, gate_ref, lg_ref, lb_ref, o_ref, rows_ref, sem):
    tm = x1_ref.shape[0]
    base = pl.program_id(0) * tm

    def issue(t, carry):
        for k in range(TOP_K):
            _row_copy(ys_hbm, dest_ref[k * n_tok + base + t], rows_ref.at[k], t, sem).start()
        return carry

    lax.fori_loop(0, tm, issue, 0)

    def drain(t, carry):
        for k in range(TOP_K):
            _row_copy(ys_hbm, 0, rows_ref.at[k], 0, sem).wait()
        return carry

    lax.fori_loop(0, tm, drain, 0)

    ff = gate_ref[:, 0:1] * rows_ref[0]
    for k in range(1, TOP_K):
        ff = ff + gate_ref[:, k:k + 1] * rows_ref[k]
    o_ref[...] = _layer_norm(alpha * x1_ref[...] + ff, lg_ref[...], lb_ref[...])


def _combine(dest_flat, ys, x1, gates_col, ln_g, ln_b, alpha):
    T, D = x1.shape
    tm = min(COMB_TM, T)
    return pl.pallas_call(
        functools.partial(_combine_kernel, alpha, T),
        grid_spec=pltpu.PrefetchScalarGridSpec(
            num_scalar_prefetch=1,
            grid=(T // tm,),
            in_specs=[pl.BlockSpec(memory_space=pl.ANY),
                      pl.BlockSpec((tm, D), lambda i, d: (i, 0)),
                      pl.BlockSpec((tm, TOP_K), lambda i, d: (i, 0)),
                      pl.BlockSpec((1, D), lambda i, d: (0, 0)),
                      pl.BlockSpec((1, D), lambda i, d: (0, 0))],
            out_specs=pl.BlockSpec((tm, D), lambda i, d: (i, 0)),
            scratch_shapes=[pltpu.VMEM((TOP_K, tm, D), F32), pltpu.SemaphoreType.DMA(())]),
        out_shape=jax.ShapeDtypeStruct((T, D), F32),
        compiler_params=_params(("arbitrary",)),
    )(dest_flat, ys, x1, gates_col, ln_g, ln_b)


def _layer(x2d, B, S, alpha, w_in, b_in, conv_w, conv_b, w_gla_gate, b_gla_gate, m_norm_w, g_norm_w,
           w_branch_m, w_branch_g, w_out, ln1_g, ln1_b, w_router, b_router, w_gu, b_gu, w_down, b_down,
           ln2_g, ln2_b):
    T, D = x2d.shape
    m_end = 2 * M_QKW + 2 * M_VW
    g_beg = m_end + 2 * M_HEADS
    g_end = g_beg + 2 * G_KW + 2 * G_VW
    mg_beg = g_end + G_RANK
    main_cols = [(0, m_end), (g_beg, g_end), (mg_beg, mg_beg + 2 * D)]
    w_main = jnp.concatenate([w_in[:, a:b] for a, b in main_cols], axis=1).astype(BF16)
    b_main = jnp.concatenate([b_in[a:b] for a, b in main_cols])[None, :]
    n_small = 2 * M_HEADS + G_RANK
    w_small = jnp.concatenate([w_in[:, m_end:g_beg], w_in[:, g_end:mg_beg],
                               jnp.zeros((D, SMALL_W - n_small), F32)], axis=1).astype(BF16)
    b_small = jnp.concatenate([b_in[m_end:g_beg], b_in[g_end:mg_beg], jnp.zeros((SMALL_W - n_small,), F32)])[None, :]

    x_bf = x2d.astype(BF16)
    proj = _proj(x_bf, w_main, b_main, PROJ_TN)
    small = _proj(x_bf, w_small, b_small, SMALL_W)
    small_t = small[:, :SUBLANES].T

    ym = _mlstm(proj, small, small_t, conv_w, conv_b[None, :], m_norm_w[None, :], B, S)
    wg_pad = jnp.zeros((SMALL_W, G_KW), F32).at[2 * M_HEADS:n_small].set(w_gla_gate).astype(BF16)
    yg = _gla(proj, small, wg_pad, b_gla_gate[None, :], g_norm_w[None, :], B, S)

    x1, top_e, gates, rank, cnt = _mix(
        ym, yg, proj, x2d, w_branch_m.astype(BF16), w_branch_g.astype(BF16), w_out.astype(BF16),
        ln1_g[None, :], ln1_b[None, :], w_router.T.astype(BF16), b_router[:, None], alpha)

    counts = cnt[:, 0].astype(jnp.int32)
    padded = (counts + EXP_TM - 1) // EXP_TM * EXP_TM
    pend = jnp.cumsum(padded)
    pstart = pend - padded
    n_tiles = (T * TOP_K) // EXP_TM + N_EXPERTS
    tile_e = jnp.searchsorted(pend, jnp.arange(n_tiles, dtype=jnp.int32) * EXP_TM, side='right')
    tile_e = jnp.minimum(tile_e, N_EXPERTS - 1).astype(jnp.int32)
    n_active = (pend[-1:] // EXP_TM).astype(jnp.int32)
    dest = (pstart[top_e] + rank).astype(jnp.int32).reshape(-1)

    xs = _dispatch(dest, x1, n_tiles * EXP_TM)
    ys = _experts(tile_e, n_active, xs, w_gu.astype(BF16), b_gu[:, None, :], w_down.astype(BF16),
                  b_down[:, None, :])
    return _combine(dest, ys, x1, gates.T, ln2_g[None, :], ln2_b[None, :], alpha)


def kernel(x, w_in, b_in, conv_w, conv_b, w_gla_gate, b_gla_gate, m_norm_w, g_norm_w, w_branch_m, w_branch_g, w_out, ln1_g, ln1_b, w_router, b_router, w_gu, b_gu, w_down, b_down, ln2_g, ln2_b):
    B, S, D = x.shape
    depth = w_in.shape[0]
    alpha = (2 * depth) ** 0.25
    x2d = x.reshape(B * S, D)
    for l in range(depth):
        x2d = _layer(x2d, B, S, alpha, w_in[l], b_in[l], conv_w[l], conv_b[l], w_gla_gate[l], b_gla_gate[l],
                     m_norm_w[l], g_norm_w[l], w_branch_m[l], w_branch_g[l], w_out[l], ln1_g[l], ln1_b[l],
                     w_router[l], b_router[l], w_gu[l], b_gu[l], w_down[l], b_down[l], ln2_g[l], ln2_b[l])
    return x2d.reshape(B, S, D)
```

```python
import functools

import jax
import jax.numpy as jnp
from jax import lax
from jax.experimental import pallas as pl
from jax.experimental.pallas import tpu as pltpu

F32 = jnp.float32
BF16 = jnp.bfloat16

M_HEADS, M_QK, M_V, CONV_W = 4, 128, 256, 4
G_HEADS, G_K, G_V, G_RANK, G_TAU = 4, 128, 256, 16, 16.0
N_EXPERTS, TOP_K = 32, 4
SWIGLU_ALPHA, SWIGLU_LIMIT = 1.702, 7.0
NORM_EPS = 1e-5
M_QKW, M_VW = M_HEADS * M_QK, M_HEADS * M_V
G_KW, G_VW = G_HEADS * G_K, G_HEADS * G_V

LANES = 128
SUBLANES = 8
VMEM_LIMIT = 56 * 1024 * 1024

PROJ_TM, PROJ_TN = 1024, 1024
SMALL_W = LANES
M_CHUNK = 256
G_STEP = 256
G_CHUNK = 64
MIX_TM = 256
EXP_TM = 512
EXP_TF = 512
DISP_TM = 512
COMB_TM = 256
ROW_DMA_UNROLL = 8


def _dot(a, b):
    return jnp.dot(a, b, preferred_element_type=F32)


def _dot_nt(a, b):
    return lax.dot_general(a, b, (((1,), (1,)), ((), ())), preferred_element_type=F32)


def _dot_tn(a, b):
    return lax.dot_general(a, b, (((0,), (0,)), ((), ())), preferred_element_type=F32)


def _split3(x):
    hi = x.astype(BF16)
    r = x - hi.astype(F32)
    mid = r.astype(BF16)
    lo = (r - mid.astype(F32)).astype(BF16)
    return hi, mid, lo


def _sigmoid(x):
    return 1.0 / (1.0 + jnp.exp(-x))


def _log_sigmoid(x):
    return jnp.minimum(x, 0.0) - jnp.log1p(jnp.exp(-jnp.abs(x)))


def _params(sem):
    return pltpu.CompilerParams(dimension_semantics=sem, vmem_limit_bytes=VMEM_LIMIT)


def _proj_kernel(x_ref, w_ref, b_ref, o_ref):
    o_ref[...] = _dot(x_ref[...], w_ref[...]) + b_ref[...]


def _proj(x_bf, w_bf, b, tn):
    T, K = x_bf.shape
    N = w_bf.shape[1]
    tm = min(PROJ_TM, T)
    return pl.pallas_call(
        _proj_kernel,
        grid=(N // tn, T // tm),
        in_specs=[pl.BlockSpec((tm, K), lambda j, i: (i, 0)),
                  pl.BlockSpec((K, tn), lambda j, i: (0, j)),
                  pl.BlockSpec((1, tn), lambda j, i: (0, j))],
        out_specs=pl.BlockSpec((tm, tn), lambda j, i: (i, j)),
        out_shape=jax.ShapeDtypeStruct((T, N), F32),
        compiler_params=_params(("parallel", "parallel")),
    )(x_bf, w_bf, b)


def _mlstm_kernel(qk_ref, v_ref, og_ref, sm_ref, smt_ref, cw_ref, cb_ref, nw_ref, o_ref,
                  ext_ref, c_ref, n_ref, m_ref):
    L = M_CHUNK
    c = pl.program_id(1)

    @pl.when(c == 0)
    def _():
        ext_ref[0:SUBLANES, :] = jnp.zeros((SUBLANES, 2 * M_QKW), F32)
        c_ref[...] = jnp.zeros_like(c_ref)
        n_ref[...] = jnp.zeros_like(n_ref)
        m_ref[...] = jnp.zeros_like(m_ref)

    ext_ref[SUBLANES:, :] = qk_ref[...]
    acc = cb_ref[...] + cw_ref[CONV_W - 1:CONV_W, :] * ext_ref[SUBLANES:, :]
    for j in range(CONV_W - 1):
        off = SUBLANES - (CONV_W - 1) + j
        acc = acc + cw_ref[j:j + 1, :] * ext_ref[off:off + L, :]
    ext_ref[0:SUBLANES, :] = ext_ref[L:L + SUBLANES, :]
    qk = acc * _sigmoid(acc)

    row = lax.broadcasted_iota(jnp.int32, (L, L), 0)
    col = lax.broadcasted_iota(jnp.int32, (L, L), 1)
    causal = row >= col
    tri = jnp.where(causal, 1.0, 0.0).astype(BF16)
    triu = jnp.where(col >= row, 1.0, 0.0).astype(BF16)

    sm = sm_ref[...]
    smt = smt_ref[...]
    h1, h2, h3 = _split3(_log_sigmoid(sm))
    bcol_all = _dot(tri, h1) + _dot(tri, h2) + _dot(tri, h3)
    r1, r2, r3 = _split3(_log_sigmoid(smt))
    brow_all = _dot(r1, triu) + _dot(r2, triu) + _dot(r3, triu)

    for h in range(M_HEADS):
        q = qk[:, h * M_QK:(h + 1) * M_QK]
        k = qk[:, M_QKW + h * M_QK:M_QKW + (h + 1) * M_QK] * (M_QK ** -0.5)
        v = v_ref[:, h * M_V:(h + 1) * M_V]
        qb, kb, vb = q.astype(BF16), k.astype(BF16), v.astype(BF16)
        li_col = sm[:, h:h + 1]
        b_col = bcol_all[:, M_HEADS + h:M_HEADS + h + 1]
        b_end = b_col[L - 1:L, :]
        li_row = smt[h:h + 1, :]
        b_row = brow_all[M_HEADS + h:M_HEADS + h + 1, :]
        m_prev = m_ref[h:h + 1, 0:1]
        n_prev = n_ref[h:h + 1, :]
        c_prev = c_ref[h]

        d_log = jnp.where(causal, b_col + (li_row - b_row), -jnp.inf)
        m_inter = b_col + m_prev
        m_t = jnp.maximum(m_inter, jnp.max(d_log, axis=-1, keepdims=True))
        dec = jnp.exp(m_inter - m_t)
        s = _dot_nt(qb, kb) * jnp.exp(d_log - m_t)
        num = _dot(s.astype(BF16), vb) + dec * _dot(qb, c_prev.astype(BF16))
        den = jnp.sum(s, axis=-1, keepdims=True) + dec * jnp.sum(q * n_prev, axis=-1, keepdims=True)
        hh = num / jnp.maximum(jnp.abs(den), jnp.exp(-m_t))

        w_end = b_end - b_col + li_col
        g_end = jnp.max(w_end, axis=0, keepdims=True)
        ke = k * jnp.exp(w_end - g_end)
        d_c = _dot_tn(ke.astype(BF16), vb)
        d_n = jnp.sum(ke, axis=0, keepdims=True)
        m_new = jnp.maximum(b_end + m_prev, g_end)
        a = jnp.exp(b_end + m_prev - m_new)
        cc = jnp.exp(g_end - m_new)
        c_ref[h] = a * c_prev + cc * d_c
        n_ref[h:h + 1, :] = a * n_prev + cc * d_n
        m_ref[h:h + 1, :] = jnp.broadcast_to(m_new, (1, LANES))

        mu = jnp.mean(hh, axis=-1, keepdims=True)
        xc = hh - mu
        var = jnp.mean(xc * xc, axis=-1, keepdims=True)
        sl = slice(h * M_V, (h + 1) * M_V)
        y = xc * lax.rsqrt(var + NORM_EPS) * nw_ref[:, sl] * _sigmoid(og_ref[:, sl])
        o_ref[:, sl] = y.astype(o_ref.dtype)


def _mlstm(proj, small, small_t, conv_w, conv_b, norm_w, B, S):
    L = M_CHUNK
    NC = S // L
    T = B * S
    rowblk = lambda b, c: (b * NC + c, 0)
    const = lambda b, c: (0, 0)
    return pl.pallas_call(
        _mlstm_kernel,
        grid=(B, NC),
        in_specs=[pl.BlockSpec((L, 2 * M_QKW), rowblk),
                  pl.BlockSpec((L, M_VW), lambda b, c: (b * NC + c, 1)),
                  pl.BlockSpec((L, M_VW), lambda b, c: (b * NC + c, 2)),
                  pl.BlockSpec((L, SMALL_W), rowblk),
                  pl.BlockSpec((SUBLANES, L), lambda b, c: (0, b * NC + c)),
                  pl.BlockSpec((CONV_W, 2 * M_QKW), const),
                  pl.BlockSpec((1, 2 * M_QKW), const),
                  pl.BlockSpec((1, M_VW), const)],
        out_specs=pl.BlockSpec((L, M_VW), rowblk),
        out_shape=jax.ShapeDtypeStruct((T, M_VW), BF16),
        scratch_shapes=[pltpu.VMEM((L + SUBLANES, 2 * M_QKW), F32),
                        pltpu.VMEM((M_HEADS, M_QK, M_V), F32),
                        pltpu.VMEM((SUBLANES, M_QK), F32),
                        pltpu.VMEM((SUBLANES, LANES), F32)],
        compiler_params=_params(("parallel", "arbitrary")),
    )(proj, proj, proj, small, small_t, conv_w, conv_b, norm_w)


def _gla_kernel(qk_ref, v_ref, gg_ref, sm_ref, wg_ref, bg_ref, nw_ref, o_ref, st_ref):
    LC = G_CHUNK
    c = pl.program_id(1)

    @pl.when(c == 0)
    def _():
        st_ref[...] = jnp.zeros_like(st_ref)

    pre = _dot(sm_ref[...].astype(BF16), wg_ref[...]) + bg_ref[...]
    log_a = _log_sigmoid(pre) / G_TAU

    row = lax.broadcasted_iota(jnp.int32, (LC, LC), 0)
    col = lax.broadcasted_iota(jnp.int32, (LC, LC), 1)
    causal = row >= col
    tri = jnp.where(causal, 1.0, 0.0).astype(BF16)

    for j in range(G_STEP // LC):
        rs = slice(j * LC, (j + 1) * LC)
        a1, a2, a3 = _split3(log_a[rs, :])
        cum = _dot(tri, a1) + _dot(tri, a2) + _dot(tri, a3)
        cum_end = cum[LC - 1:LC, :]
        e_q = jnp.exp(cum)
        e_k = jnp.exp(-cum)
        e_end = jnp.exp(cum_end - cum)
        e_dec = jnp.exp(cum_end)
        for h in range(G_HEADS):
            ks = slice(h * G_K, (h + 1) * G_K)
            vs = slice(h * G_V, (h + 1) * G_V)
            q = qk_ref[rs, ks] * (G_K ** -0.5)
            k = qk_ref[rs, G_KW + h * G_K:G_KW + (h + 1) * G_K]
            vb = v_ref[rs, vs].astype(BF16)
            q_in = (q * e_q[:, ks]).astype(BF16)
            k_in = (k * e_k[:, ks]).astype(BF16)
            k_end = (k * e_end[:, ks]).astype(BF16)
            st = st_ref[h]
            s = jnp.where(causal, _dot_nt(q_in, k_in), 0.0)
            o = _dot(s.astype(BF16), vb) + _dot_nt(q_in, st.astype(BF16))
            st_ref[h] = st * e_dec[:, ks] + _dot_tn(vb, k_end)
            ms = jnp.mean(o * o, axis=-1, keepdims=True)
            g = gg_ref[rs, vs]
            y = o * lax.rsqrt(ms + NORM_EPS) * nw_ref[:, vs] * (g * _sigmoid(g))
            o_ref[rs, vs] = y.astype(o_ref.dtype)


def _gla(proj, small, wg_pad, bg, norm_w, B, S):
    L = G_STEP
    NC = S // L
    T = B * S
    rowblk = lambda b, c: (b * NC + c, 0)
    const = lambda b, c: (0, 0)
    return pl.pallas_call(
        _gla_kernel,
        grid=(B, NC),
        in_specs=[pl.BlockSpec((L, 2 * G_KW), lambda b, c: (b * NC + c, 3)),
                  pl.BlockSpec((L, G_VW), lambda b, c: (b * NC + c, 4)),
                  pl.BlockSpec((L, G_VW), lambda b, c: (b * NC + c, 5)),
                  pl.BlockSpec((L, SMALL_W), rowblk),
                  pl.BlockSpec((SMALL_W, G_KW), const),
                  pl.BlockSpec((1, G_KW), const),
                  pl.BlockSpec((1, G_VW), const)],
        out_specs=pl.BlockSpec((L, G_VW), rowblk),
        out_shape=jax.ShapeDtypeStruct((T, G_VW), BF16),
        scratch_shapes=[pltpu.VMEM((G_HEADS, G_V, G_K), F32)],
        compiler_params=_params(("parallel", "arbitrary")),
    )(proj, proj, proj, small, wg_pad, bg, norm_w)


def _layer_norm(r, g, b):
    mu = jnp.mean(r, axis=-1, keepdims=True)
    xc = r - mu
    var = jnp.mean(xc * xc, axis=-1, keepdims=True)
    return xc * lax.rsqrt(var + NORM_EPS) * g + b


def _mix_kernel(alpha, ym_ref, yg_ref, gm_ref, gg_ref, x_ref, wbm_ref, wbg_ref, wout_ref, lg_ref, lb_ref,
                wr_ref, br_ref, x1_ref, tope_ref, gate_ref, rank_ref, cnt_ref, carry_ref):
    tm = x_ref.shape[0]
    i = pl.program_id(0)

    @pl.when(i == 0)
    def _():
        carry_ref[...] = jnp.zeros_like(carry_ref)

    pm = _dot(ym_ref[...], wbm_ref[...])
    pg = _dot(yg_ref[...], wbg_ref[...])
    z = _sigmoid(gm_ref[...]) * pm + _sigmoid(gg_ref[...]) * pg
    mix = _dot(z.astype(BF16), wout_ref[...])
    x1 = _layer_norm(alpha * x_ref[...] + mix, lg_ref[...], lb_ref[...])
    x1_ref[...] = x1

    logits = _dot_nt(wr_ref[...], x1.astype(BF16)) + br_ref[...]
    eidx = lax.broadcasted_iota(jnp.int32, (N_EXPERTS, tm), 0)
    vals, hots = [], []
    cur = logits
    for k in range(TOP_K):
        mx = jnp.max(cur, axis=0, keepdims=True)
        idx = jnp.min(jnp.where(cur == mx, eidx, N_EXPERTS), axis=0, keepdims=True)
        hot = eidx == idx
        cur = jnp.where(hot, -jnp.inf, cur)
        vals.append(mx)
        hots.append(hot)
        tope_ref[k:k + 1, :] = idx
    exps = [jnp.exp(v - vals[0]) for v in vals]
    tot = exps[0] + exps[1] + exps[2] + exps[3]
    for k in range(TOP_K):
        gate_ref[k:k + 1, :] = exps[k] / tot

    sel = jnp.zeros((N_EXPERTS, tm), F32)
    for hot in hots:
        sel = sel + jnp.where(hot, 1.0, 0.0)
    row = lax.broadcasted_iota(jnp.int32, (tm, tm), 0)
    col = lax.broadcasted_iota(jnp.int32, (tm, tm), 1)
    before = jnp.where(row < col, 1.0, 0.0).astype(BF16)
    pos = _dot(sel.astype(BF16), before) + carry_ref[:, 0:1]
    for k in range(TOP_K):
        rk = jnp.sum(jnp.where(hots[k], pos, 0.0), axis=0, keepdims=True)
        rank_ref[k:k + 1, :] = rk.astype(jnp.int32)
    carry_ref[...] = carry_ref[...] + jnp.sum(sel, axis=1, keepdims=True)
    cnt_ref[...] = carry_ref[...]


def _mix(ym, yg, proj, x2d, wbm, wbg, wout, ln_g, ln_b, wr_t, br, alpha):
    T, D = x2d.shape
    tm = min(MIX_TM, T)
    rowblk = lambda i: (i, 0)
    const = lambda i: (0, 0)
    once = dict(pipeline_mode=pl.Buffered(1))
    return pl.pallas_call(
        functools.partial(_mix_kernel, alpha),
        grid=(T // tm,),
        in_specs=[pl.BlockSpec((tm, M_VW), rowblk),
                  pl.BlockSpec((tm, G_VW), rowblk),
                  pl.BlockSpec((tm, D), lambda i: (i, 3)),
                  pl.BlockSpec((tm, D), lambda i: (i, 4)),
                  pl.BlockSpec((tm, D), rowblk),
                  pl.BlockSpec((M_VW, D), const, **once),
                  pl.BlockSpec((G_VW, D), const, **once),
                  pl.BlockSpec((D, D), const, **once),
                  pl.BlockSpec((1, D), const),
                  pl.BlockSpec((1, D), const),
                  pl.BlockSpec((N_EXPERTS, D), const),
                  pl.BlockSpec((N_EXPERTS, 1), const)],
        out_specs=[pl.BlockSpec((tm, D), rowblk),
                   pl.BlockSpec((TOP_K, tm), lambda i: (0, i)),
                   pl.BlockSpec((TOP_K, tm), lambda i: (0, i)),
                   pl.BlockSpec((TOP_K, tm), lambda i: (0, i)),
                   pl.BlockSpec((N_EXPERTS, LANES), const)],
        out_shape=[jax.ShapeDtypeStruct((T, D), F32),
                   jax.ShapeDtypeStruct((TOP_K, T), jnp.int32),
                   jax.ShapeDtypeStruct((TOP_K, T), F32),
                   jax.ShapeDtypeStruct((TOP_K, T), jnp.int32),
                   jax.ShapeDtypeStruct((N_EXPERTS, LANES), F32)],
        scratch_shapes=[pltpu.VMEM((N_EXPERTS, LANES), F32)],
        compiler_params=_params(("arbitrary",)),
    )(ym, yg, proj, proj, x2d, wbm, wbg, wout, ln_g, ln_b, wr_t, br)


def _row_copy(src, src_row, dst, dst_row, sem):
    return pltpu.make_async_copy(src.at[pl.ds(src_row, 1)], dst.at[pl.ds(dst_row, 1)], sem)


def _dispatch_kernel(n_tok, dest_ref, x1_ref, xs_in_hbm, xs_hbm, sem):
    del xs_in_hbm
    tm = x1_ref.shape[0]
    base = pl.program_id(0) * tm

    def issue(t, carry):
        for k in range(TOP_K):
            _row_copy(x1_ref, t, xs_hbm, dest_ref[k * n_tok + base + t], sem).start()
        return carry

    lax.fori_loop(0, tm, issue, 0, unroll=ROW_DMA_UNROLL)
    for k in range(TOP_K):
        pltpu.make_async_copy(x1_ref, xs_hbm.at[pl.ds(0, tm)], sem).wait()


def _dispatch(dest_flat, x1, n_rows):
    T, D = x1.shape
    tm = min(DISP_TM, T)
    zeros = jnp.zeros((n_rows, D), x1.dtype)
    return pl.pallas_call(
        functools.partial(_dispatch_kernel, T),
        grid_spec=pltpu.PrefetchScalarGridSpec(
            num_scalar_prefetch=1,
            grid=(T // tm,),
            in_specs=[pl.BlockSpec((tm, D), lambda i, d: (i, 0)), pl.BlockSpec(memory_space=pl.ANY)],
            out_specs=pl.BlockSpec(memory_space=pl.ANY),
            scratch_shapes=[pltpu.SemaphoreType.DMA(())]),
        out_shape=jax.ShapeDtypeStruct((n_rows, D), x1.dtype),
        input_output_aliases={2: 0},
        compiler_params=pltpu.CompilerParams(dimension_semantics=("arbitrary",), has_side_effects=True),
    )(dest_flat, x1, zeros)


def _expert_kernel(te_ref, na_ref, xs_ref, wg_ref, wl_ref, bg_ref, bl_ref, wd_ref, bd_ref, o_ref):
    del te_ref
    i = pl.program_id(0)
    f = pl.program_id(1)

    @pl.when(i < na_ref[0])
    def _():
        a = xs_ref[...].astype(BF16)
        g = _dot(a, wg_ref[...]) + bg_ref[...]
        l = _dot(a, wl_ref[...]) + bl_ref[...]
        a_glu = jnp.minimum(g, SWIGLU_LIMIT)
        a_lin = jnp.clip(l, -SWIGLU_LIMIT, SWIGLU_LIMIT)
        act = a_glu * _sigmoid(SWIGLU_ALPHA * a_glu) * (a_lin + 1.0)
        part = _dot(act.astype(BF16), wd_ref[...])

        @pl.when(f == 0)
        def _():
            o_ref[...] = part + bd_ref[...]

        @pl.when(f > 0)
        def _():
            o_ref[...] += part

    @pl.when(jnp.logical_and(i >= na_ref[0], f == 0))
    def _():
        o_ref[...] = jnp.zeros_like(o_ref)


def _experts(tile_e, n_active, xs, w_gu, b_gu, w_down, b_down):
    P, D = xs.shape
    E, _, F2 = w_gu.shape
    Fh = F2 // 2
    tf = min(EXP_TF, Fh)
    nf = Fh // tf
    n_tiles = P // EXP_TM

    def tile(i, na):
        return jnp.minimum(i, na[0] - 1)

    def fsel(i, f, na):
        return jnp.where(i < na[0], f, nf - 1)

    return pl.pallas_call(
        _expert_kernel,
        grid_spec=pltpu.PrefetchScalarGridSpec(
            num_scalar_prefetch=2,
            grid=(n_tiles, nf),
            in_specs=[
                pl.BlockSpec((EXP_TM, D), lambda i, f, te, na: (tile(i, na), 0)),
                pl.BlockSpec((None, D, tf), lambda i, f, te, na: (te[tile(i, na)], 0, fsel(i, f, na))),
                pl.BlockSpec((None, D, tf), lambda i, f, te, na: (te[tile(i, na)], 0, nf + fsel(i, f, na))),
                pl.BlockSpec((None, 1, tf), lambda i, f, te, na: (te[tile(i, na)], 0, fsel(i, f, na))),
                pl.BlockSpec((None, 1, tf), lambda i, f, te, na: (te[tile(i, na)], 0, nf + fsel(i, f, na))),
                pl.BlockSpec((None, tf, D), lambda i, f, te, na: (te[tile(i, na)], fsel(i, f, na), 0)),
                pl.BlockSpec((None, 1, D), lambda i, f, te, na: (te[tile(i, na)], 0, 0)),
            ],
            out_specs=pl.BlockSpec((EXP_TM, D), lambda i, f, te, na: (i, 0))),
        out_shape=jax.ShapeDtypeStruct((P, D), F32),
        compiler_params=_params(("arbitrary", "arbitrary")),
    )(tile_e, n_active, xs, w_gu, w_gu, b_gu, b_gu, w_down, b_down)


def _combine_kernel(alpha, n_tok, dest_ref, ys_hbm, x1_ref, gate_ref, lg_ref, lb_ref, o_ref, rows_ref, sem):
    tm = x1_ref.shape[0]
    base = pl.program_id(0) * tm

    def issue(t, carry):
        for k in range(TOP_K):
            _row_copy(ys_hbm, dest_ref[k * n_tok + base + t], rows_ref.at[k], t, sem).start()
        return carry

    lax.fori_loop(0, tm, issue, 0, unroll=ROW_DMA_UNROLL)
    for k in range(TOP_K):
        pltpu.make_async_copy(ys_hbm.at[pl.ds(0, tm)], rows_ref.at[k], sem).wait()

    ff = gate_ref[:, 0:1] * rows_ref[0]
    for k in range(1, TOP_K):
        ff = ff + gate_ref[:, k:k + 1] * rows_ref[k]
    o_ref[...] = _layer_norm(alpha * x1_ref[...] + ff, lg_ref[...], lb_ref[...])


def _combine(dest_flat, ys, x1, gates_col, ln_g, ln_b, alpha):
    T, D = x1.shape
    tm = min(COMB_TM, T)
    return pl.pallas_call(
        functools.partial(_combine_kernel, alpha, T),
        grid_spec=pltpu.PrefetchScalarGridSpec(
            num_scalar_prefetch=1,
            grid=(T // tm,),
            in_specs=[pl.BlockSpec(memory_space=pl.ANY),
                      pl.BlockSpec((tm, D), lambda i, d: (i, 0)),
                      pl.BlockSpec((tm, TOP_K), lambda i, d: (i, 0)),
                      pl.BlockSpec((1, D), lambda i, d: (0, 0)),
                      pl.BlockSpec((1, D), lambda i, d: (0, 0))],
            out_specs=pl.BlockSpec((tm, D), lambda i, d: (i, 0)),
            scratch_shapes=[pltpu.VMEM((TOP_K, tm, D), F32), pltpu.SemaphoreType.DMA(())]),
        out_shape=jax.ShapeDtypeStruct((T, D), F32),
        compiler_params=_params(("arbitrary",)),
    )(dest_flat, ys, x1, gates_col, ln_g, ln_b)


def _layer(x2d, B, S, alpha, w_in, b_in, conv_w, conv_b, w_gla_gate, b_gla_gate, m_norm_w, g_norm_w,
           w_branch_m, w_branch_g, w_out, ln1_g, ln1_b, w_router, b_router, w_gu, b_gu, w_down, b_down,
           ln2_g, ln2_b):
    T, D = x2d.shape
    m_end = 2 * M_QKW + 2 * M_VW
    g_beg = m_end + 2 * M_HEADS
    g_end = g_beg + 2 * G_KW + 2 * G_VW
    mg_beg = g_end + G_RANK
    main_cols = [(0, m_end), (g_beg, g_end), (mg_beg, mg_beg + 2 * D)]
    w_main = jnp.concatenate([w_in[:, a:b] for a, b in main_cols], axis=1).astype(BF16)
    b_main = jnp.concatenate([b_in[a:b] for a, b in main_cols])[None, :]
    n_small = 2 * M_HEADS + G_RANK
    w_small = jnp.concatenate([w_in[:, m_end:g_beg], w_in[:, g_end:mg_beg],
                               jnp.zeros((D, SMALL_W - n_small), F32)], axis=1).astype(BF16)
    b_small = jnp.concatenate([b_in[m_end:g_beg], b_in[g_end:mg_beg], jnp.zeros((SMALL_W - n_small,), F32)])[None, :]

    x_bf = x2d.astype(BF16)
    proj = _proj(x_bf, w_main, b_main, PROJ_TN)
    small = _proj(x_bf, w_small, b_small, SMALL_W)
    small_t = small[:, :SUBLANES].T

    ym = _mlstm(proj, small, small_t, conv_w, conv_b[None, :], m_norm_w[None, :], B, S)
    wg_pad = jnp.zeros((SMALL_W, G_KW), F32).at[2 * M_HEADS:n_small].set(w_gla_gate).astype(BF16)
    yg = _gla(proj, small, wg_pad, b_gla_gate[None, :], g_norm_w[None, :], B, S)

    x1, top_e, gates, rank, cnt = _mix(
        ym, yg, proj, x2d, w_branch_m.astype(BF16), w_branch_g.astype(BF16), w_out.astype(BF16),
        ln1_g[None, :], ln1_b[None, :], w_router.T.astype(BF16), b_router[:, None], alpha)

    counts = cnt[:, 0].astype(jnp.int32)
    padded = (counts + EXP_TM - 1) // EXP_TM * EXP_TM
    pend = jnp.cumsum(padded)
    pstart = pend - padded
    n_tiles = (T * TOP_K) // EXP_TM + N_EXPERTS
    tile_start = jnp.arange(n_tiles, dtype=jnp.int32) * EXP_TM
    tile_e = jnp.sum((pend[None, :] <= tile_start[:, None]).astype(jnp.int32), axis=1)
    tile_e = jnp.minimum(tile_e, N_EXPERTS - 1)
    n_active = (pend[-1:] // EXP_TM).astype(jnp.int32)
    dest = rank
    for e in range(N_EXPERTS):
        dest = dest + jnp.where(top_e == e, pstart[e], 0)
    dest = dest.reshape(-1)

    xs = _dispatch(dest, x1, n_tiles * EXP_TM)
    ys = _experts(tile_e, n_active, xs, w_gu.astype(BF16), b_gu[:, None, :], w_down.astype(BF16),
                  b_down[:, None, :])
    return _combine(dest, ys, x1, gates.T, ln2_g[None, :], ln2_b[None, :], alpha)


def kernel(x, w_in, b_in, conv_w, conv_b, w_gla_gate, b_gla_gate, m_norm_w, g_norm_w, w_branch_m, w_branch_g, w_out, ln1_g, ln1_b, w_router, b_router, w_gu, b_gu, w_down, b_down, ln2_g, ln2_b):
    B, S, D = x.shape
    depth = w_in.shape[0]
    alpha = (2 * depth) ** 0.25
    x2d = x.reshape(B * S, D)
    for l in range(depth):
        x2d = _layer(x2d, B, S, alpha, w_in[l], b_in[l], conv_w[l], conv_b[l], w_gla_gate[l], b_gla_gate[l],
                     m_norm_w[l], g_norm_w[l], w_branch_m[l], w_branch_g[l], w_out[l], ln1_g[l], ln1_b[l],
                     w_router[l], b_router[l], w_gu[l], b_gu[l], w_down[l], b_down[l], ln2_g[l], ln2_b[l])
    return x2d.reshape(B, S, D)
```

```python
import functools

import jax
import jax.numpy as jnp
from jax import lax
from jax.experimental import pallas as pl
from jax.experimental.pallas import tpu as pltpu

F32 = jnp.float32
BF16 = jnp.bfloat16

M_HEADS, M_QK, M_V, CONV_W = 4, 128, 256, 4
G_HEADS, G_K, G_V, G_RANK, G_TAU = 4, 128, 256, 16, 16.0
N_EXPERTS, TOP_K = 32, 4
SWIGLU_ALPHA, SWIGLU_LIMIT = 1.702, 7.0
NORM_EPS = 1e-5
M_QKW, M_VW = M_HEADS * M_QK, M_HEADS * M_V
G_KW, G_VW = G_HEADS * G_K, G_HEADS * G_V

LANES = 128
SUBLANES = 8
VMEM_LIMIT = 56 * 1024 * 1024

PROJ_TM, PROJ_TN = 1024, 1024
SMALL_W = LANES
M_CHUNK = 256
G_STEP = 256
G_CHUNK = 64
MIX_TM = 256
EXP_TM = 512
EXP_TF = 1024
DISP_TM = 512
COMB_TM = 256
ROW_DMA_UNROLL = 8


def _dot(a, b):
    return jnp.dot(a, b, preferred_element_type=F32)


def _dot_nt(a, b):
    return lax.dot_general(a, b, (((1,), (1,)), ((), ())), preferred_element_type=F32)


def _dot_tn(a, b):
    return lax.dot_general(a, b, (((0,), (0,)), ((), ())), preferred_element_type=F32)


def _split3(x):
    hi = x.astype(BF16)
    r = x - hi.astype(F32)
    mid = r.astype(BF16)
    lo = (r - mid.astype(F32)).astype(BF16)
    return hi, mid, lo


def _sigmoid(x):
    return 1.0 / (1.0 + jnp.exp(-x))


def _log_sigmoid(x):
    return jnp.minimum(x, 0.0) - jnp.log1p(jnp.exp(-jnp.abs(x)))


def _params(sem):
    return pltpu.CompilerParams(dimension_semantics=sem, vmem_limit_bytes=VMEM_LIMIT)


def _proj_kernel(x_ref, w_ref, b_ref, ws_ref, bs_ref, o_ref, os_ref, xb_ref):
    @pl.when(pl.program_id(1) == 0)
    def _():
        xb_ref[...] = x_ref[...].astype(BF16)
        os_ref[...] = _dot(xb_ref[...], ws_ref[...]) + bs_ref[...]

    o_ref[...] = _dot(xb_ref[...], w_ref[...]) + b_ref[...]


def _proj(x, w_bf, b, ws_bf, bs):
    T, K = x.shape
    N = w_bf.shape[1]
    tm = min(PROJ_TM, T)
    tn = PROJ_TN
    return pl.pallas_call(
        _proj_kernel,
        grid=(T // tm, N // tn),
        in_specs=[pl.BlockSpec((tm, K), lambda i, j: (i, 0)),
                  pl.BlockSpec((K, tn), lambda i, j: (0, j)),
                  pl.BlockSpec((1, tn), lambda i, j: (0, j)),
                  pl.BlockSpec((K, SMALL_W), lambda i, j: (0, 0)),
                  pl.BlockSpec((1, SMALL_W), lambda i, j: (0, 0))],
        out_specs=[pl.BlockSpec((tm, tn), lambda i, j: (i, j)),
                   pl.BlockSpec((tm, SMALL_W), lambda i, j: (i, 0))],
        out_shape=[jax.ShapeDtypeStruct((T, N), F32), jax.ShapeDtypeStruct((T, SMALL_W), F32)],
        scratch_shapes=[pltpu.VMEM((tm, K), BF16)],
        compiler_params=_params(("parallel", "arbitrary")),
    )(x, w_bf, b, ws_bf, bs)


def _mlstm_kernel(qk_ref, v_ref, og_ref, sm_ref, smt_ref, cw_ref, cb_ref, nw_ref, o_ref,
                  ext_ref, c_ref, n_ref, m_ref):
    L = M_CHUNK
    c = pl.program_id(1)

    @pl.when(c == 0)
    def _():
        ext_ref[0:SUBLANES, :] = jnp.zeros((SUBLANES, 2 * M_QKW), F32)
        c_ref[...] = jnp.zeros_like(c_ref)
        n_ref[...] = jnp.zeros_like(n_ref)
        m_ref[...] = jnp.zeros_like(m_ref)

    ext_ref[SUBLANES:, :] = qk_ref[...]
    acc = cb_ref[...] + cw_ref[CONV_W - 1:CONV_W, :] * ext_ref[SUBLANES:, :]
    for j in range(CONV_W - 1):
        off = SUBLANES - (CONV_W - 1) + j
        acc = acc + cw_ref[j:j + 1, :] * ext_ref[off:off + L, :]
    ext_ref[0:SUBLANES, :] = ext_ref[L:L + SUBLANES, :]
    qk = acc * _sigmoid(acc)

    row = lax.broadcasted_iota(jnp.int32, (L, L), 0)
    col = lax.broadcasted_iota(jnp.int32, (L, L), 1)
    causal = row >= col
    tri = jnp.where(causal, 1.0, 0.0).astype(BF16)
    triu = jnp.where(col >= row, 1.0, 0.0).astype(BF16)

    sm = sm_ref[...]
    smt = smt_ref[...]
    h1, h2, h3 = _split3(_log_sigmoid(sm))
    bcol_all = _dot(tri, h1) + _dot(tri, h2) + _dot(tri, h3)
    r1, r2, r3 = _split3(_log_sigmoid(smt))
    brow_all = _dot(r1, triu) + _dot(r2, triu) + _dot(r3, triu)

    for h in range(M_HEADS):
        q = qk[:, h * M_QK:(h + 1) * M_QK]
        k = qk[:, M_QKW + h * M_QK:M_QKW + (h + 1) * M_QK] * (M_QK ** -0.5)
        v = v_ref[:, h * M_V:(h + 1) * M_V]
        qb, kb, vb = q.astype(BF16), k.astype(BF16), v.astype(BF16)
        li_col = sm[:, h:h + 1]
        b_col = bcol_all[:, M_HEADS + h:M_HEADS + h + 1]
        b_end = b_col[L - 1:L, :]
        li_row = smt[h:h + 1, :]
        b_row = brow_all[M_HEADS + h:M_HEADS + h + 1, :]
        m_prev = m_ref[h:h + 1, 0:1]
        n_prev = n_ref[h:h + 1, :]
        c_prev = c_ref[h]

        d_log = jnp.where(causal, b_col + (li_row - b_row), -jnp.inf)
        m_inter = b_col + m_prev
        m_t = jnp.maximum(m_inter, jnp.max(d_log, axis=-1, keepdims=True))
        dec = jnp.exp(m_inter - m_t)
        s = _dot_nt(qb, kb) * jnp.exp(d_log - m_t)
        num = _dot(s.astype(BF16), vb) + dec * _dot(qb, c_prev.astype(BF16))
        den = jnp.sum(s, axis=-1, keepdims=True) + dec * jnp.sum(q * n_prev, axis=-1, keepdims=True)
        hh = num / jnp.maximum(jnp.abs(den), jnp.exp(-m_t))

        w_end = b_end - b_col + li_col
        g_end = jnp.max(w_end, axis=0, keepdims=True)
        ke = k * jnp.exp(w_end - g_end)
        d_c = _dot_tn(ke.astype(BF16), vb)
        d_n = jnp.sum(ke, axis=0, keepdims=True)
        m_new = jnp.maximum(b_end + m_prev, g_end)
        a = jnp.exp(b_end + m_prev - m_new)
        cc = jnp.exp(g_end - m_new)
        c_ref[h] = a * c_prev + cc * d_c
        n_ref[h:h + 1, :] = a * n_prev + cc * d_n
        m_ref[h:h + 1, :] = jnp.broadcast_to(m_new, (1, LANES))

        mu = jnp.mean(hh, axis=-1, keepdims=True)
        xc = hh - mu
        var = jnp.mean(xc * xc, axis=-1, keepdims=True)
        sl = slice(h * M_V, (h + 1) * M_V)
        y = xc * lax.rsqrt(var + NORM_EPS) * nw_ref[:, sl] * _sigmoid(og_ref[:, sl])
        o_ref[:, sl] = y.astype(o_ref.dtype)


def _mlstm(proj, small, small_t, conv_w, conv_b, norm_w, B, S):
    L = M_CHUNK
    NC = S // L
    T = B * S
    rowblk = lambda b, c: (b * NC + c, 0)
    const = lambda b, c: (0, 0)
    return pl.pallas_call(
        _mlstm_kernel,
        grid=(B, NC),
        in_specs=[pl.BlockSpec((L, 2 * M_QKW), rowblk),
                  pl.BlockSpec((L, M_VW), lambda b, c: (b * NC + c, 1)),
                  pl.BlockSpec((L, M_VW), lambda b, c: (b * NC + c, 2)),
                  pl.BlockSpec((L, SMALL_W), rowblk),
                  pl.BlockSpec((SUBLANES, L), lambda b, c: (0, b * NC + c)),
                  pl.BlockSpec((CONV_W, 2 * M_QKW), const),
                  pl.BlockSpec((1, 2 * M_QKW), const),
                  pl.BlockSpec((1, M_VW), const)],
        out_specs=pl.BlockSpec((L, M_VW), rowblk),
        out_shape=jax.ShapeDtypeStruct((T, M_VW), BF16),
        scratch_shapes=[pltpu.VMEM((L + SUBLANES, 2 * M_QKW), F32),
                        pltpu.VMEM((M_HEADS, M_QK, M_V), F32),
                        pltpu.VMEM((SUBLANES, M_QK), F32),
                        pltpu.VMEM((SUBLANES, LANES), F32)],
        compiler_params=_params(("parallel", "arbitrary")),
    )(proj, proj, proj, small, small_t, conv_w, conv_b, norm_w)


def _gla_kernel(qk_ref, v_ref, gg_ref, sm_ref, wg_ref, bg_ref, nw_ref, o_ref, st_ref):
    LC = G_CHUNK
    c = pl.program_id(1)

    @pl.when(c == 0)
    def _():
        st_ref[...] = jnp.zeros_like(st_ref)

    pre = _dot(sm_ref[...].astype(BF16), wg_ref[...]) + bg_ref[...]
    log_a = _log_sigmoid(pre) / G_TAU

    row = lax.broadcasted_iota(jnp.int32, (LC, LC), 0)
    col = lax.broadcasted_iota(jnp.int32, (LC, LC), 1)
    causal = row >= col
    tri = jnp.where(causal, 1.0, 0.0).astype(BF16)

    for j in range(G_STEP // LC):
        rs = slice(j * LC, (j + 1) * LC)
        a1, a2, a3 = _split3(log_a[rs, :])
        cum = _dot(tri, a1) + _dot(tri, a2) + _dot(tri, a3)
        cum_end = cum[LC - 1:LC, :]
        e_q = jnp.exp(cum)
        e_k = jnp.exp(-cum)
        e_end = jnp.exp(cum_end - cum)
        e_dec = jnp.exp(cum_end)
        for h in range(G_HEADS):
            ks = slice(h * G_K, (h + 1) * G_K)
            vs = slice(h * G_V, (h + 1) * G_V)
            q = qk_ref[rs, ks] * (G_K ** -0.5)
            k = qk_ref[rs, G_KW + h * G_K:G_KW + (h + 1) * G_K]
            vb = v_ref[rs, vs].astype(BF16)
            q_in = (q * e_q[:, ks]).astype(BF16)
            k_in = (k * e_k[:, ks]).astype(BF16)
            k_end = (k * e_end[:, ks]).astype(BF16)
            st = st_ref[h]
            s = jnp.where(causal, _dot_nt(q_in, k_in), 0.0)
            o = _dot(s.astype(BF16), vb) + _dot_nt(q_in, st.astype(BF16))
            st_ref[h] = st * e_dec[:, ks] + _dot_tn(vb, k_end)
            ms = jnp.mean(o * o, axis=-1, keepdims=True)
            g = gg_ref[rs, vs]
            y = o * lax.rsqrt(ms + NORM_EPS) * nw_ref[:, vs] * (g * _sigmoid(g))
            o_ref[rs, vs] = y.astype(o_ref.dtype)


def _gla(proj, small, wg_pad, bg, norm_w, B, S):
    L = G_STEP
    NC = S // L
    T = B * S
    rowblk = lambda b, c: (b * NC + c, 0)
    const = lambda b, c: (0, 0)
    return pl.pallas_call(
        _gla_kernel,
        grid=(B, NC),
        in_specs=[pl.BlockSpec((L, 2 * G_KW), lambda b, c: (b * NC + c, 3)),
                  pl.BlockSpec((L, G_VW), lambda b, c: (b * NC + c, 4)),
                  pl.BlockSpec((L, G_VW), lambda b, c: (b * NC + c, 5)),
                  pl.BlockSpec((L, SMALL_W), rowblk),
                  pl.BlockSpec((SMALL_W, G_KW), const),
                  pl.BlockSpec((1, G_KW), const),
                  pl.BlockSpec((1, G_VW), const)],
        out_specs=pl.BlockSpec((L, G_VW), rowblk),
        out_shape=jax.ShapeDtypeStruct((T, G_VW), BF16),
        scratch_shapes=[pltpu.VMEM((G_HEADS, G_V, G_K), F32)],
        compiler_params=_params(("parallel", "arbitrary")),
    )(proj, proj, proj, small, wg_pad, bg, norm_w)


def _layer_norm(r, g, b):
    mu = jnp.mean(r, axis=-1, keepdims=True)
    xc = r - mu
    var = jnp.mean(xc * xc, axis=-1, keepdims=True)
    return xc * lax.rsqrt(var + NORM_EPS) * g + b


def _mix_kernel(alpha, ym_ref, yg_ref, gm_ref, gg_ref, x_ref, wbm_ref, wbg_ref, wout_ref, lg_ref, lb_ref,
                wr_ref, br_ref, x1_ref, tope_ref, gate_ref, rank_ref, cnt_ref, carry_ref):
    tm = x_ref.shape[0]
    i = pl.program_id(0)

    @pl.when(i == 0)
    def _():
        carry_ref[...] = jnp.zeros_like(carry_ref)

    pm = _dot(ym_ref[...], wbm_ref[...])
    pg = _dot(yg_ref[...], wbg_ref[...])
    z = _sigmoid(gm_ref[...]) * pm + _sigmoid(gg_ref[...]) * pg
    mix = _dot(z.astype(BF16), wout_ref[...])
    x1 = _layer_norm(alpha * x_ref[...] + mix, lg_ref[...], lb_ref[...])
    x1_ref[...] = x1

    logits = _dot_nt(wr_ref[...], x1.astype(BF16)) + br_ref[...]
    eidx = lax.broadcasted_iota(jnp.int32, (N_EXPERTS, tm), 0)
    vals, hots = [], []
    cur = logits
    for k in range(TOP_K):
        mx = jnp.max(cur, axis=0, keepdims=True)
        idx = jnp.min(jnp.where(cur == mx, eidx, N_EXPERTS), axis=0, keepdims=True)
        hot = eidx == idx
        cur = jnp.where(hot, -jnp.inf, cur)
        vals.append(mx)
        hots.append(hot)
        tope_ref[k:k + 1, :] = idx
    exps = [jnp.exp(v - vals[0]) for v in vals]
    tot = exps[0] + exps[1] + exps[2] + exps[3]
    for k in range(TOP_K):
        gate_ref[k:k + 1, :] = exps[k] / tot

    sel = jnp.zeros((N_EXPERTS, tm), F32)
    for hot in hots:
        sel = sel + jnp.where(hot, 1.0, 0.0)
    row = lax.broadcasted_iota(jnp.int32, (tm, tm), 0)
    col = lax.broadcasted_iota(jnp.int32, (tm, tm), 1)
    before = jnp.where(row < col, 1.0, 0.0).astype(BF16)
    pos = _dot(sel.astype(BF16), before) + carry_ref[:, 0:1]
    for k in range(TOP_K):
        rk = jnp.sum(jnp.where(hots[k], pos, 0.0), axis=0, keepdims=True)
        rank_ref[k:k + 1, :] = rk.astype(jnp.int32)
    carry_ref[...] = carry_ref[...] + jnp.sum(sel, axis=1, keepdims=True)
    cnt_ref[...] = carry_ref[...]


def _mix(ym, yg, proj, x2d, wbm, wbg, wout, ln_g, ln_b, wr_t, br, alpha):
    T, D = x2d.shape
    tm = min(MIX_TM, T)
    rowblk = lambda i: (i, 0)
    const = lambda i: (0, 0)
    once = dict(pipeline_mode=pl.Buffered(1))
    return pl.pallas_call(
        functools.partial(_mix_kernel, alpha),
        grid=(T // tm,),
        in_specs=[pl.BlockSpec((tm, M_VW), rowblk),
                  pl.BlockSpec((tm, G_VW), rowblk),
                  pl.BlockSpec((tm, D), lambda i: (i, 3)),
                  pl.BlockSpec((tm, D), lambda i: (i, 4)),
                  pl.BlockSpec((tm, D), rowblk),
                  pl.BlockSpec((M_VW, D), const, **once),
                  pl.BlockSpec((G_VW, D), const, **once),
                  pl.BlockSpec((D, D), const, **once),
                  pl.BlockSpec((1, D), const),
                  pl.BlockSpec((1, D), const),
                  pl.BlockSpec((N_EXPERTS, D), const),
                  pl.BlockSpec((N_EXPERTS, 1), const)],
        out_specs=[pl.BlockSpec((tm, D), rowblk),
                   pl.BlockSpec((TOP_K, tm), lambda i: (0, i)),
                   pl.BlockSpec((TOP_K, tm), lambda i: (0, i)),
                   pl.BlockSpec((TOP_K, tm), lambda i: (0, i)),
                   pl.BlockSpec((N_EXPERTS, LANES), const)],
        out_shape=[jax.ShapeDtypeStruct((T, D), F32),
                   jax.ShapeDtypeStruct((TOP_K, T), jnp.int32),
                   jax.ShapeDtypeStruct((TOP_K, T), F32),
                   jax.ShapeDtypeStruct((TOP_K, T), jnp.int32),
                   jax.ShapeDtypeStruct((N_EXPERTS, LANES), F32)],
        scratch_shapes=[pltpu.VMEM((N_EXPERTS, LANES), F32)],
        compiler_params=_params(("arbitrary",)),
    )(ym, yg, proj, proj, x2d, wbm, wbg, wout, ln_g, ln_b, wr_t, br)


def _row_copy(src, src_row, dst, dst_row, sem):
    return pltpu.make_async_copy(src.at[pl.ds(src_row, 1)], dst.at[pl.ds(dst_row, 1)], sem)


def _dispatch_kernel(n_tok, dest_ref, x1_ref, xs_hbm, sem):
    tm = x1_ref.shape[0]
    base = pl.program_id(0) * tm

    def issue(t, carry):
        for k in range(TOP_K):
            _row_copy(x1_ref, t, xs_hbm, dest_ref[k * n_tok + base + t], sem).start()
        return carry

    lax.fori_loop(0, tm, issue, 0, unroll=ROW_DMA_UNROLL)
    for k in range(TOP_K):
        pltpu.make_async_copy(x1_ref, xs_hbm.at[pl.ds(0, tm)], sem).wait()


def _dispatch(dest_flat, x1, n_rows):
    T, D = x1.shape
    tm = min(DISP_TM, T)
    return pl.pallas_call(
        functools.partial(_dispatch_kernel, T),
        grid_spec=pltpu.PrefetchScalarGridSpec(
            num_scalar_prefetch=1,
            grid=(T // tm,),
            in_specs=[pl.BlockSpec((tm, D), lambda i, d: (i, 0))],
            out_specs=pl.BlockSpec(memory_space=pl.ANY),
            scratch_shapes=[pltpu.SemaphoreType.DMA(())]),
        out_shape=jax.ShapeDtypeStruct((n_rows, D), x1.dtype),
        compiler_params=pltpu.CompilerParams(dimension_semantics=("arbitrary",), has_side_effects=True),
    )(dest_flat, x1)


def _zero_rows_kernel(start_ref, cnt_ref, xs_in_hbm, xs_hbm, zrow_ref, sem):
    del xs_in_hbm
    zrow_ref[...] = jnp.zeros_like(zrow_ref)

    def per_range(e, carry):
        start, cnt = start_ref[e], cnt_ref[e]

        def issue(r, c):
            _row_copy(zrow_ref, 0, xs_hbm, start + r, sem).start()
            return c

        def drain(r, c):
            _row_copy(zrow_ref, 0, xs_hbm, 0, sem).wait()
            return c

        lax.fori_loop(0, cnt, issue, 0)
        lax.fori_loop(0, cnt, drain, 0)
        return carry

    lax.fori_loop(0, start_ref.shape[0], per_range, 0)


def _zero_rows(start, cnt, xs):
    return pl.pallas_call(
        _zero_rows_kernel,
        grid_spec=pltpu.PrefetchScalarGridSpec(
            num_scalar_prefetch=2,
            grid=(1,),
            in_specs=[pl.BlockSpec(memory_space=pl.ANY)],
            out_specs=pl.BlockSpec(memory_space=pl.ANY),
            scratch_shapes=[pltpu.VMEM((SUBLANES, xs.shape[1]), xs.dtype), pltpu.SemaphoreType.DMA(())]),
        out_shape=jax.ShapeDtypeStruct(xs.shape, xs.dtype),
        input_output_aliases={2: 0},
        compiler_params=pltpu.CompilerParams(dimension_semantics=("arbitrary",), has_side_effects=True),
    )(start, cnt, xs)


def _expert_kernel(te_ref, na_ref, xs_ref, wg_ref, wl_ref, bg_ref, bl_ref, wd_ref, bd_ref, o_ref):
    del te_ref
    i = pl.program_id(0)
    f = pl.program_id(1)

    @pl.when(i < na_ref[0])
    def _():
        a = xs_ref[...].astype(BF16)
        g = _dot(a, wg_ref[...]) + bg_ref[...]
        l = _dot(a, wl_ref[...]) + bl_ref[...]
        a_glu = jnp.minimum(g, SWIGLU_LIMIT)
        a_lin = jnp.clip(l, -SWIGLU_LIMIT, SWIGLU_LIMIT)
        act = a_glu * _sigmoid(SWIGLU_ALPHA * a_glu) * (a_lin + 1.0)
        part = _dot(act.astype(BF16), wd_ref[...])

        @pl.when(f == 0)
        def _():
            o_ref[...] = part + bd_ref[...]

        @pl.when(f > 0)
        def _():
            o_ref[...] += part

    @pl.when(jnp.logical_and(i >= na_ref[0], f == 0))
    def _():
        o_ref[...] = jnp.zeros_like(o_ref)


def _experts(tile_e, n_active, xs, w_gu, b_gu, w_down, b_down):
    P, D = xs.shape
    E, _, F2 = w_gu.shape
    Fh = F2 // 2
    tf = min(EXP_TF, Fh)
    nf = Fh // tf
    n_tiles = P // EXP_TM

    def tile(i, na):
        return jnp.minimum(i, na[0] - 1)

    def fsel(i, f, na):
        return jnp.where(i < na[0], f, nf - 1)

    return pl.pallas_call(
        _expert_kernel,
        grid_spec=pltpu.PrefetchScalarGridSpec(
            num_scalar_prefetch=2,
            grid=(n_tiles, nf),
            in_specs=[
                pl.BlockSpec((EXP_TM, D), lambda i, f, te, na: (tile(i, na), 0)),
                pl.BlockSpec((None, D, tf), lambda i, f, te, na: (te[tile(i, na)], 0, fsel(i, f, na))),
                pl.BlockSpec((None, D, tf), lambda i, f, te, na: (te[tile(i, na)], 0, nf + fsel(i, f, na))),
                pl.BlockSpec((None, 1, tf), lambda i, f, te, na: (te[tile(i, na)], 0, fsel(i, f, na))),
                pl.BlockSpec((None, 1, tf), lambda i, f, te, na: (te[tile(i, na)], 0, nf + fsel(i, f, na))),
                pl.BlockSpec((None, tf, D), lambda i, f, te, na: (te[tile(i, na)], fsel(i, f, na), 0)),
                pl.BlockSpec((None, 1, D), lambda i, f, te, na: (te[tile(i, na)], 0, 0)),
            ],
            out_specs=pl.BlockSpec((EXP_TM, D), lambda i, f, te, na: (i, 0))),
        out_shape=jax.ShapeDtypeStruct((P, D), F32),
        compiler_params=_params(("arbitrary", "arbitrary")),
    )(tile_e, n_active, xs, w_gu, w_gu, b_gu, b_gu, w_down, b_down)


def _combine_kernel(alpha, n_tok, dest_ref, ys_hbm, x1_ref, gate_ref, lg_ref, lb_ref, o_ref, rows_ref, sem):
    tm = x1_ref.shape[0]
    base = pl.program_id(0) * tm

    def issue(t, carry):
        for k in range(TOP_K):
            _row_copy(ys_hbm, dest_ref[k * n_tok + base + t], rows_ref.at[k], t, sem).start()
        return carry

    lax.fori_loop(0, tm, issue, 0, unroll=ROW_DMA_UNROLL)
    for k in range(TOP_K):
        pltpu.make_async_copy(ys_hbm.at[pl.ds(0, tm)], rows_ref.at[k], sem).wait()

    ff = gate_ref[:, 0:1] * rows_ref[0]
    for k in range(1, TOP_K):
        ff = ff + gate_ref[:, k:k + 1] * rows_ref[k]
    o_ref[...] = _layer_norm(alpha * x1_ref[...] + ff, lg_ref[...], lb_ref[...])


def _combine(dest_flat, ys, x1, gates_col, ln_g, ln_b, alpha):
    T, D = x1.shape
    tm = min(COMB_TM, T)
    return pl.pallas_call(
        functools.partial(_combine_kernel, alpha, T),
        grid_spec=pltpu.PrefetchScalarGridSpec(
            num_scalar_prefetch=1,
            grid=(T // tm,),
            in_specs=[pl.BlockSpec(memory_space=pl.ANY),
                      pl.BlockSpec((tm, D), lambda i, d: (i, 0)),
                      pl.BlockSpec((tm, TOP_K), lambda i, d: (i, 0)),
                      pl.BlockSpec((1, D), lambda i, d: (0, 0)),
                      pl.BlockSpec((1, D), lambda i, d: (0, 0))],
            out_specs=pl.BlockSpec((tm, D), lambda i, d: (i, 0)),
            scratch_shapes=[pltpu.VMEM((TOP_K, tm, D), F32), pltpu.SemaphoreType.DMA(())]),
        out_shape=jax.ShapeDtypeStruct((T, D), F32),
        compiler_params=_params(("arbitrary",)),
    )(dest_flat, ys, x1, gates_col, ln_g, ln_b)


def _layer(x2d, B, S, alpha, w_in, b_in, conv_w, conv_b, w_gla_gate, b_gla_gate, m_norm_w, g_norm_w,
           w_branch_m, w_branch_g, w_out, ln1_g, ln1_b, w_router, b_router, w_gu, b_gu, w_down, b_down,
           ln2_g, ln2_b):
    T, D = x2d.shape
    m_end = 2 * M_QKW + 2 * M_VW
    g_beg = m_end + 2 * M_HEADS
    g_end = g_beg + 2 * G_KW + 2 * G_VW
    mg_beg = g_end + G_RANK
    main_cols = [(0, m_end), (g_beg, g_end), (mg_beg, mg_beg + 2 * D)]
    w_main = jnp.concatenate([w_in[:, a:b] for a, b in main_cols], axis=1).astype(BF16)
    b_main = jnp.concatenate([b_in[a:b] for a, b in main_cols])[None, :]
    n_small = 2 * M_HEADS + G_RANK
    w_small = jnp.concatenate([w_in[:, m_end:g_beg], w_in[:, g_end:mg_beg],
                               jnp.zeros((D, SMALL_W - n_small), F32)], axis=1).astype(BF16)
    b_small = jnp.concatenate([b_in[m_end:g_beg], b_in[g_end:mg_beg], jnp.zeros((SMALL_W - n_small,), F32)])[None, :]

    proj, small = _proj(x2d, w_main, b_main, w_small, b_small)
    small_t = small[:, :SUBLANES].T

    ym = _mlstm(proj, small, small_t, conv_w, conv_b[None, :], m_norm_w[None, :], B, S)
    wg_pad = jnp.zeros((SMALL_W, G_KW), F32).at[2 * M_HEADS:n_small].set(w_gla_gate).astype(BF16)
    yg = _gla(proj, small, wg_pad, b_gla_gate[None, :], g_norm_w[None, :], B, S)

    x1, top_e, gates, rank, cnt = _mix(
        ym, yg, proj, x2d, w_branch_m.astype(BF16), w_branch_g.astype(BF16), w_out.astype(BF16),
        ln1_g[None, :], ln1_b[None, :], w_router.T.astype(BF16), b_router[:, None], alpha)

    counts = cnt[:, 0].astype(jnp.int32)
    padded = (counts + EXP_TM - 1) // EXP_TM * EXP_TM
    pend = jnp.cumsum(padded)
    pstart = pend - padded
    n_tiles = (T * TOP_K) // EXP_TM + N_EXPERTS
    tile_start = jnp.arange(n_tiles, dtype=jnp.int32) * EXP_TM
    tile_e = jnp.sum((pend[None, :] <= tile_start[:, None]).astype(jnp.int32), axis=1)
    tile_e = jnp.minimum(tile_e, N_EXPERTS - 1)
    n_active = (pend[-1:] // EXP_TM).astype(jnp.int32)
    dest = rank
    for e in range(N_EXPERTS):
        dest = dest + jnp.where(top_e == e, pstart[e], 0)
    dest = dest.reshape(-1)

    xs = _dispatch(dest, x1, n_tiles * EXP_TM)
    n_rows = n_tiles * EXP_TM
    hole_start = jnp.concatenate([pstart + counts, pend[-1:]])
    hole_cnt = jnp.concatenate([padded - counts, n_rows - pend[-1:]])
    xs = _zero_rows(hole_start, hole_cnt, xs)
    ys = _experts(tile_e, n_active, xs, w_gu.astype(BF16), b_gu[:, None, :], w_down.astype(BF16),
                  b_down[:, None, :])
    return _combine(dest, ys, x1, gates.T, ln2_g[None, :], ln2_b[None, :], alpha)


def kernel(x, w_in, b_in, conv_w, conv_b, w_gla_gate, b_gla_gate, m_norm_w, g_norm_w, w_branch_m, w_branch_g, w_out, ln1_g, ln1_b, w_router, b_router, w_gu, b_gu, w_down, b_down, ln2_g, ln2_b):
    B, S, D = x.shape
    depth = w_in.shape[0]
    alpha = (2 * depth) ** 0.25
    x2d = x.reshape(B * S, D)
    for l in range(depth):
        x2d = _layer(x2d, B, S, alpha, w_in[l], b_in[l], conv_w[l], conv_b[l], w_gla_gate[l], b_gla_gate[l],
                     m_norm_w[l], g_norm_w[l], w_branch_m[l], w_branch_g[l], w_out[l], ln1_g[l], ln1_b[l],
                     w_router[l], b_router[l], w_gu[l], b_gu[l], w_down[l], b_down[l], ln2_g[l], ln2_b[l])
    return x2d.reshape(B, S, D)
```

```python
import functools

import jax
import jax.numpy as jnp
from jax import lax
from jax.experimental import pallas as pl
from jax.experimental.pallas import tpu as pltpu

F32 = jnp.float32
BF16 = jnp.bfloat16

M_HEADS, M_QK, M_V, CONV_W = 4, 128, 256, 4
G_HEADS, G_K, G_V, G_RANK, G_TAU = 4, 128, 256, 16, 16.0
N_EXPERTS, TOP_K = 32, 4
SWIGLU_ALPHA, SWIGLU_LIMIT = 1.702, 7.0
NORM_EPS = 1e-5
M_QKW, M_VW = M_HEADS * M_QK, M_HEADS * M_V
G_KW, G_VW = G_HEADS * G_K, G_HEADS * G_V

LANES = 128
SUBLANES = 8
VMEM_LIMIT = 56 * 1024 * 1024

PROJ_TM, PROJ_TN = 1024, 1024
SMALL_W = LANES
M_CHUNK = 256
G_STEP = 256
G_CHUNK = 64
MIX_TM = 256
EXP_TM = 512
EXP_TF = 1024
DISP_TM = 512
COMB_TM = 256
ROW_DMA_UNROLL = 8
ZERO_CHUNK = 64


def _dot(a, b):
    return jnp.dot(a, b, preferred_element_type=F32)


def _dot_nt(a, b):
    return lax.dot_general(a, b, (((1,), (1,)), ((), ())), preferred_element_type=F32)


def _dot_tn(a, b):
    return lax.dot_general(a, b, (((0,), (0,)), ((), ())), preferred_element_type=F32)


def _split3(x):
    hi = x.astype(BF16)
    r = x - hi.astype(F32)
    mid = r.astype(BF16)
    lo = (r - mid.astype(F32)).astype(BF16)
    return hi, mid, lo


def _sigmoid(x):
    return 1.0 / (1.0 + jnp.exp(-x))


def _log_sigmoid(x):
    return jnp.minimum(x, 0.0) - jnp.log1p(jnp.exp(-jnp.abs(x)))


def _params(sem):
    return pltpu.CompilerParams(dimension_semantics=sem, vmem_limit_bytes=VMEM_LIMIT)


def _proj_kernel(x_ref, w_ref, b_ref, ws_ref, bs_ref, o_ref, os_ref, xb_ref):
    @pl.when(pl.program_id(1) == 0)
    def _():
        xb_ref[...] = x_ref[...].astype(BF16)
        os_ref[...] = _dot(xb_ref[...], ws_ref[...]) + bs_ref[...]

    o_ref[...] = _dot(xb_ref[...], w_ref[...]) + b_ref[...]


def _proj(x, w_bf, b, ws_bf, bs):
    T, K = x.shape
    N = w_bf.shape[1]
    tm = min(PROJ_TM, T)
    tn = PROJ_TN
    return pl.pallas_call(
        _proj_kernel,
        grid=(T // tm, N // tn),
        in_specs=[pl.BlockSpec((tm, K), lambda i, j: (i, 0)),
                  pl.BlockSpec((K, tn), lambda i, j: (0, j)),
                  pl.BlockSpec((1, tn), lambda i, j: (0, j)),
                  pl.BlockSpec((K, SMALL_W), lambda i, j: (0, 0)),
                  pl.BlockSpec((1, SMALL_W), lambda i, j: (0, 0))],
        out_specs=[pl.BlockSpec((tm, tn), lambda i, j: (i, j)),
                   pl.BlockSpec((tm, SMALL_W), lambda i, j: (i, 0))],
        out_shape=[jax.ShapeDtypeStruct((T, N), F32), jax.ShapeDtypeStruct((T, SMALL_W), F32)],
        scratch_shapes=[pltpu.VMEM((tm, K), BF16)],
        compiler_params=_params(("parallel", "arbitrary")),
    )(x, w_bf, b, ws_bf, bs)


def _mlstm_kernel(qk_ref, v_ref, og_ref, sm_ref, smt_ref, cw_ref, cb_ref, nw_ref, o_ref,
                  ext_ref, c_ref, n_ref, m_ref):
    L = M_CHUNK
    c = pl.program_id(1)

    @pl.when(c == 0)
    def _():
        ext_ref[0:SUBLANES, :] = jnp.zeros((SUBLANES, 2 * M_QKW), F32)
        c_ref[...] = jnp.zeros_like(c_ref)
        n_ref[...] = jnp.zeros_like(n_ref)
        m_ref[...] = jnp.zeros_like(m_ref)

    ext_ref[SUBLANES:, :] = qk_ref[...]
    acc = cb_ref[...] + cw_ref[CONV_W - 1:CONV_W, :] * ext_ref[SUBLANES:, :]
    for j in range(CONV_W - 1):
        off = SUBLANES - (CONV_W - 1) + j
        acc = acc + cw_ref[j:j + 1, :] * ext_ref[off:off + L, :]
    ext_ref[0:SUBLANES, :] = ext_ref[L:L + SUBLANES, :]
    qk = acc * _sigmoid(acc)

    row = lax.broadcasted_iota(jnp.int32, (L, L), 0)
    col = lax.broadcasted_iota(jnp.int32, (L, L), 1)
    causal = row >= col
    tri = jnp.where(causal, 1.0, 0.0).astype(BF16)
    triu = jnp.where(col >= row, 1.0, 0.0).astype(BF16)

    sm = sm_ref[...]
    smt = smt_ref[...]
    h1, h2, h3 = _split3(_log_sigmoid(sm))
    bcol_all = _dot(tri, h1) + _dot(tri, h2) + _dot(tri, h3)
    r1, r2, r3 = _split3(_log_sigmoid(smt))
    brow_all = _dot(r1, triu) + _dot(r2, triu) + _dot(r3, triu)

    for h in range(M_HEADS):
        q = qk[:, h * M_QK:(h + 1) * M_QK]
        k = qk[:, M_QKW + h * M_QK:M_QKW + (h + 1) * M_QK] * (M_QK ** -0.5)
        v = v_ref[:, h * M_V:(h + 1) * M_V]
        qb, kb, vb = q.astype(BF16), k.astype(BF16), v.astype(BF16)
        li_col = sm[:, h:h + 1]
        b_col = bcol_all[:, M_HEADS + h:M_HEADS + h + 1]
        b_end = b_col[L - 1:L, :]
        li_row = smt[h:h + 1, :]
        b_row = brow_all[M_HEADS + h:M_HEADS + h + 1, :]
        m_prev = m_ref[h:h + 1, 0:1]
        n_prev = n_ref[h:h + 1, :]
        c_prev = c_ref[h]

        d_log = jnp.where(causal, b_col + (li_row - b_row), -jnp.inf)
        m_inter = b_col + m_prev
        m_t = jnp.maximum(m_inter, jnp.max(d_log, axis=-1, keepdims=True))
        dec = jnp.exp(m_inter - m_t)
        s = _dot_nt(qb, kb) * jnp.exp(d_log - m_t)
        num = _dot(s.astype(BF16), vb) + dec * _dot(qb, c_prev.astype(BF16))
        den = jnp.sum(s, axis=-1, keepdims=True) + dec * jnp.sum(q * n_prev, axis=-1, keepdims=True)
        hh = num / jnp.maximum(jnp.abs(den), jnp.exp(-m_t))

        w_end = b_end - b_col + li_col
        g_end = jnp.max(w_end, axis=0, keepdims=True)
        ke = k * jnp.exp(w_end - g_end)
        d_c = _dot_tn(ke.astype(BF16), vb)
        d_n = jnp.sum(ke, axis=0, keepdims=True)
        m_new = jnp.maximum(b_end + m_prev, g_end)
        a = jnp.exp(b_end + m_prev - m_new)
        cc = jnp.exp(g_end - m_new)
        c_ref[h] = a * c_prev + cc * d_c
        n_ref[h:h + 1, :] = a * n_prev + cc * d_n
        m_ref[h:h + 1, :] = jnp.broadcast_to(m_new, (1, LANES))

        mu = jnp.mean(hh, axis=-1, keepdims=True)
        xc = hh - mu
        var = jnp.mean(xc * xc, axis=-1, keepdims=True)
        sl = slice(h * M_V, (h + 1) * M_V)
        y = xc * lax.rsqrt(var + NORM_EPS) * nw_ref[:, sl] * _sigmoid(og_ref[:, sl])
        o_ref[:, sl] = y.astype(o_ref.dtype)


def _mlstm(proj, small, small_t, conv_w, conv_b, norm_w, B, S):
    L = M_CHUNK
    NC = S // L
    T = B * S
    rowblk = lambda b, c: (b * NC + c, 0)
    const = lambda b, c: (0, 0)
    return pl.pallas_call(
        _mlstm_kernel,
        grid=(B, NC),
        in_specs=[pl.BlockSpec((L, 2 * M_QKW), rowblk),
                  pl.BlockSpec((L, M_VW), lambda b, c: (b * NC + c, 1)),
                  pl.BlockSpec((L, M_VW), lambda b, c: (b * NC + c, 2)),
                  pl.BlockSpec((L, SMALL_W), rowblk),
                  pl.BlockSpec((SUBLANES, L), lambda b, c: (0, b * NC + c)),
                  pl.BlockSpec((CONV_W, 2 * M_QKW), const),
                  pl.BlockSpec((1, 2 * M_QKW), const),
                  pl.BlockSpec((1, M_VW), const)],
        out_specs=pl.BlockSpec((L, M_VW), rowblk),
        out_shape=jax.ShapeDtypeStruct((T, M_VW), BF16),
        scratch_shapes=[pltpu.VMEM((L + SUBLANES, 2 * M_QKW), F32),
                        pltpu.VMEM((M_HEADS, M_QK, M_V), F32),
                        pltpu.VMEM((SUBLANES, M_QK), F32),
                        pltpu.VMEM((SUBLANES, LANES), F32)],
        compiler_params=_params(("parallel", "arbitrary")),
    )(proj, proj, proj, small, small_t, conv_w, conv_b, norm_w)


def _gla_kernel(qk_ref, v_ref, gg_ref, sm_ref, wg_ref, bg_ref, nw_ref, o_ref, st_ref):
    LC = G_CHUNK
    c = pl.program_id(1)

    @pl.when(c == 0)
    def _():
        st_ref[...] = jnp.zeros_like(st_ref)

    pre = _dot(sm_ref[...].astype(BF16), wg_ref[...]) + bg_ref[...]
    log_a = _log_sigmoid(pre) / G_TAU

    row = lax.broadcasted_iota(jnp.int32, (LC, LC), 0)
    col = lax.broadcasted_iota(jnp.int32, (LC, LC), 1)
    causal = row >= col
    tri = jnp.where(causal, 1.0, 0.0).astype(BF16)

    for j in range(G_STEP // LC):
        rs = slice(j * LC, (j + 1) * LC)
        a1, a2, a3 = _split3(log_a[rs, :])
        cum = _dot(tri, a1) + _dot(tri, a2) + _dot(tri, a3)
        cum_end = cum[LC - 1:LC, :]
        e_q = jnp.exp(cum)
        e_k = jnp.exp(-cum)
        e_end = jnp.exp(cum_end - cum)
        e_dec = jnp.exp(cum_end)
        for h in range(G_HEADS):
            ks = slice(h * G_K, (h + 1) * G_K)
            vs = slice(h * G_V, (h + 1) * G_V)
            q = qk_ref[rs, ks] * (G_K ** -0.5)
            k = qk_ref[rs, G_KW + h * G_K:G_KW + (h + 1) * G_K]
            vb = v_ref[rs, vs].astype(BF16)
            q_in = (q * e_q[:, ks]).astype(BF16)
            k_in = (k * e_k[:, ks]).astype(BF16)
            k_end = (k * e_end[:, ks]).astype(BF16)
            st = st_ref[h]
            s = jnp.where(causal, _dot_nt(q_in, k_in), 0.0)
            o = _dot(s.astype(BF16), vb) + _dot_nt(q_in, st.astype(BF16))
            st_ref[h] = st * e_dec[:, ks] + _dot_tn(vb, k_end)
            ms = jnp.mean(o * o, axis=-1, keepdims=True)
            g = gg_ref[rs, vs]
            y = o * lax.rsqrt(ms + NORM_EPS) * nw_ref[:, vs] * (g * _sigmoid(g))
            o_ref[rs, vs] = y.astype(o_ref.dtype)


def _gla(proj, small, wg_pad, bg, norm_w, B, S):
    L = G_STEP
    NC = S // L
    T = B * S
    rowblk = lambda b, c: (b * NC + c, 0)
    const = lambda b, c: (0, 0)
    return pl.pallas_call(
        _gla_kernel,
        grid=(B, NC),
        in_specs=[pl.BlockSpec((L, 2 * G_KW), lambda b, c: (b * NC + c, 3)),
                  pl.BlockSpec((L, G_VW), lambda b, c: (b * NC + c, 4)),
                  pl.BlockSpec((L, G_VW), lambda b, c: (b * NC + c, 5)),
                  pl.BlockSpec((L, SMALL_W), rowblk),
                  pl.BlockSpec((SMALL_W, G_KW), const),
                  pl.BlockSpec((1, G_KW), const),
                  pl.BlockSpec((1, G_VW), const)],
        out_specs=pl.BlockSpec((L, G_VW), rowblk),
        out_shape=jax.ShapeDtypeStruct((T, G_VW), BF16),
        scratch_shapes=[pltpu.VMEM((G_HEADS, G_V, G_K), F32)],
        compiler_params=_params(("parallel", "arbitrary")),
    )(proj, proj, proj, small, wg_pad, bg, norm_w)


def _layer_norm(r, g, b):
    mu = jnp.mean(r, axis=-1, keepdims=True)
    xc = r - mu
    var = jnp.mean(xc * xc, axis=-1, keepdims=True)
    return xc * lax.rsqrt(var + NORM_EPS) * g + b


def _mix_kernel(alpha, ym_ref, yg_ref, gm_ref, gg_ref, x_ref, wbm_ref, wbg_ref, wout_ref, lg_ref, lb_ref,
                wr_ref, br_ref, x1_ref, tope_ref, gate_ref, rank_ref, cnt_ref, carry_ref):
    tm = x_ref.shape[0]
    i = pl.program_id(0)

    @pl.when(i == 0)
    def _():
        carry_ref[...] = jnp.zeros_like(carry_ref)

    pm = _dot(ym_ref[...], wbm_ref[...])
    pg = _dot(yg_ref[...], wbg_ref[...])
    z = _sigmoid(gm_ref[...]) * pm + _sigmoid(gg_ref[...]) * pg
    mix = _dot(z.astype(BF16), wout_ref[...])
    x1 = _layer_norm(alpha * x_ref[...] + mix, lg_ref[...], lb_ref[...])
    x1_ref[...] = x1

    logits = _dot_nt(wr_ref[...], x1.astype(BF16)) + br_ref[...]
    eidx = lax.broadcasted_iota(jnp.int32, (N_EXPERTS, tm), 0)
    vals, hots = [], []
    cur = logits
    for k in range(TOP_K):
        mx = jnp.max(cur, axis=0, keepdims=True)
        idx = jnp.min(jnp.where(cur == mx, eidx, N_EXPERTS), axis=0, keepdims=True)
        hot = eidx == idx
        cur = jnp.where(hot, -jnp.inf, cur)
        vals.append(mx)
        hots.append(hot)
        tope_ref[k:k + 1, :] = idx
    exps = [jnp.exp(v - vals[0]) for v in vals]
    tot = exps[0] + exps[1] + exps[2] + exps[3]
    for k in range(TOP_K):
        gate_ref[k:k + 1, :] = exps[k] / tot

    sel = jnp.zeros((N_EXPERTS, tm), F32)
    for hot in hots:
        sel = sel + jnp.where(hot, 1.0, 0.0)
    row = lax.broadcasted_iota(jnp.int32, (tm, tm), 0)
    col = lax.broadcasted_iota(jnp.int32, (tm, tm), 1)
    before = jnp.where(row < col, 1.0, 0.0).astype(BF16)
    pos = _dot(sel.astype(BF16), before) + carry_ref[:, 0:1]
    for k in range(TOP_K):
        rk = jnp.sum(jnp.where(hots[k], pos, 0.0), axis=0, keepdims=True)
        rank_ref[k:k + 1, :] = rk.astype(jnp.int32)
    carry_ref[...] = carry_ref[...] + jnp.sum(sel, axis=1, keepdims=True)
    cnt_ref[...] = carry_ref[...]


def _mix(ym, yg, proj, x2d, wbm, wbg, wout, ln_g, ln_b, wr_t, br, alpha):
    T, D = x2d.shape
    tm = min(MIX_TM, T)
    rowblk = lambda i: (i, 0)
    const = lambda i: (0, 0)
    once = dict(pipeline_mode=pl.Buffered(1))
    return pl.pallas_call(
        functools.partial(_mix_kernel, alpha),
        grid=(T // tm,),
        in_specs=[pl.BlockSpec((tm, M_VW), rowblk),
                  pl.BlockSpec((tm, G_VW), rowblk),
                  pl.BlockSpec((tm, D), lambda i: (i, 3)),
                  pl.BlockSpec((tm, D), lambda i: (i, 4)),
                  pl.BlockSpec((tm, D), rowblk),
                  pl.BlockSpec((M_VW, D), const, **once),
                  pl.BlockSpec((G_VW, D), const, **once),
                  pl.BlockSpec((D, D), const, **once),
                  pl.BlockSpec((1, D), const),
                  pl.BlockSpec((1, D), const),
                  pl.BlockSpec((N_EXPERTS, D), const),
                  pl.BlockSpec((N_EXPERTS, 1), const)],
        out_specs=[pl.BlockSpec((tm, D), rowblk),
                   pl.BlockSpec((TOP_K, tm), lambda i: (0, i)),
                   pl.BlockSpec((TOP_K, tm), lambda i: (0, i)),
                   pl.BlockSpec((TOP_K, tm), lambda i: (0, i)),
                   pl.BlockSpec((N_EXPERTS, LANES), const)],
        out_shape=[jax.ShapeDtypeStruct((T, D), F32),
                   jax.ShapeDtypeStruct((TOP_K, T), jnp.int32),
                   jax.ShapeDtypeStruct((TOP_K, T), F32),
                   jax.ShapeDtypeStruct((TOP_K, T), jnp.int32),
                   jax.ShapeDtypeStruct((N_EXPERTS, LANES), F32)],
        scratch_shapes=[pltpu.VMEM((N_EXPERTS, LANES), F32)],
        compiler_params=_params(("arbitrary",)),
    )(ym, yg, proj, proj, x2d, wbm, wbg, wout, ln_g, ln_b, wr_t, br)


def _row_copy(src, src_row, dst, dst_row, sem):
    return pltpu.make_async_copy(src.at[pl.ds(src_row, 1)], dst.at[pl.ds(dst_row, 1)], sem)


def _dispatch_kernel(n_tok, dest_ref, x1_ref, xs_hbm, sem):
    tm = x1_ref.shape[0]
    base = pl.program_id(0) * tm

    def issue(t, carry):
        for k in range(TOP_K):
            _row_copy(x1_ref, t, xs_hbm, dest_ref[k * n_tok + base + t], sem).start()
        return carry

    lax.fori_loop(0, tm, issue, 0, unroll=ROW_DMA_UNROLL)
    for k in range(TOP_K):
        pltpu.make_async_copy(x1_ref, xs_hbm.at[pl.ds(0, tm)], sem).wait()


def _dispatch(dest_flat, x1, n_rows):
    T, D = x1.shape
    tm = min(DISP_TM, T)
    return pl.pallas_call(
        functools.partial(_dispatch_kernel, T),
        grid_spec=pltpu.PrefetchScalarGridSpec(
            num_scalar_prefetch=1,
            grid=(T // tm,),
            in_specs=[pl.BlockSpec((tm, D), lambda i, d: (i, 0))],
            out_specs=pl.BlockSpec(memory_space=pl.ANY),
            scratch_shapes=[pltpu.SemaphoreType.DMA(())]),
        out_shape=jax.ShapeDtypeStruct((n_rows, D), x1.dtype),
        compiler_params=pltpu.CompilerParams(dimension_semantics=("arbitrary",), has_side_effects=True),
    )(dest_flat, x1)


def _zero_rows_kernel(start_ref, cnt_ref, xs_in_hbm, xs_hbm, zrow_ref, sem):
    del xs_in_hbm
    zrow_ref[...] = jnp.zeros_like(zrow_ref)

    def per_range(e, carry):
        start, cnt = start_ref[e], cnt_ref[e]
        head = jnp.minimum(cnt, (-start) & (SUBLANES - 1))
        mid = start + head
        n_blk = (cnt - head) // ZERO_CHUNK
        tail = mid + n_blk * ZERO_CHUNK
        n_tail = cnt - head - n_blk * ZERO_CHUNK

        def blk_copy(r):
            dst = pl.multiple_of(mid + r * ZERO_CHUNK, SUBLANES)
            return pltpu.make_async_copy(zrow_ref, xs_hbm.at[pl.ds(dst, ZERO_CHUNK)], sem)

        def rows(first, n):
            def issue(r, c):
                _row_copy(zrow_ref, 0, xs_hbm, first + r, sem).start()
                return c

            def drain(r, c):
                _row_copy(zrow_ref, 0, xs_hbm, 0, sem).wait()
                return c

            lax.fori_loop(0, n, issue, 0)
            lax.fori_loop(0, n, drain, 0)

        def issue_blk(r, c):
            blk_copy(r).start()
            return c

        def drain_blk(r, c):
            blk_copy(0).wait()
            return c

        rows(start, head)
        lax.fori_loop(0, n_blk, issue_blk, 0)
        lax.fori_loop(0, n_blk, drain_blk, 0)
        rows(tail, n_tail)
        return carry

    lax.fori_loop(0, start_ref.shape[0], per_range, 0)


def _zero_rows(start, cnt, xs):
    return pl.pallas_call(
        _zero_rows_kernel,
        grid_spec=pltpu.PrefetchScalarGridSpec(
            num_scalar_prefetch=2,
            grid=(1,),
            in_specs=[pl.BlockSpec(memory_space=pl.ANY)],
            out_specs=pl.BlockSpec(memory_space=pl.ANY),
            scratch_shapes=[pltpu.VMEM((ZERO_CHUNK, xs.shape[1]), xs.dtype), pltpu.SemaphoreType.DMA(())]),
        out_shape=jax.ShapeDtypeStruct(xs.shape, xs.dtype),
        input_output_aliases={2: 0},
        compiler_params=pltpu.CompilerParams(dimension_semantics=("arbitrary",), has_side_effects=True),
    )(start, cnt, xs)


def _expert_kernel(te_ref, na_ref, xs_ref, wgu_ref, bgu_ref, wd_ref, bd_ref, o_ref):
    del te_ref
    i = pl.program_id(0)
    fh = wd_ref.shape[0]
    tf = min(EXP_TF, fh)

    @pl.when(i < na_ref[0])
    def _():
        a = xs_ref[...].astype(BF16)
        acc = None
        for f in range(fh // tf):
            glu = slice(f * tf, (f + 1) * tf)
            lin = slice(fh + f * tf, fh + (f + 1) * tf)
            g = _dot(a, wgu_ref[:, glu]) + bgu_ref[:, glu]
            l = _dot(a, wgu_ref[:, lin]) + bgu_ref[:, lin]
            a_glu = jnp.minimum(g, SWIGLU_LIMIT)
            a_lin = jnp.clip(l, -SWIGLU_LIMIT, SWIGLU_LIMIT)
            act = a_glu * _sigmoid(SWIGLU_ALPHA * a_glu) * (a_lin + 1.0)
            part = _dot(act.astype(BF16), wd_ref[glu, :])
            acc = part if acc is None else acc + part
        o_ref[...] = acc + bd_ref[...]

    @pl.when(i >= na_ref[0])
    def _():
        o_ref[...] = jnp.zeros_like(o_ref)


def _experts(tile_e, n_active, xs, w_gu, b_gu, w_down, b_down):
    P, D = xs.shape
    E, _, F2 = w_gu.shape
    Fh = F2 // 2
    n_tiles = P // EXP_TM

    def tile(i, na):
        return jnp.minimum(i, na[0] - 1)

    once = dict(pipeline_mode=pl.Buffered(1))
    return pl.pallas_call(
        _expert_kernel,
        grid_spec=pltpu.PrefetchScalarGridSpec(
            num_scalar_prefetch=2,
            grid=(n_tiles,),
            in_specs=[
                pl.BlockSpec((EXP_TM, D), lambda i, te, na: (tile(i, na), 0)),
                pl.BlockSpec((None, D, F2), lambda i, te, na: (te[tile(i, na)], 0, 0), **once),
                pl.BlockSpec((None, 1, F2), lambda i, te, na: (te[tile(i, na)], 0, 0)),
                pl.BlockSpec((None, Fh, D), lambda i, te, na: (te[tile(i, na)], 0, 0), **once),
                pl.BlockSpec((None, 1, D), lambda i, te, na: (te[tile(i, na)], 0, 0)),
            ],
            out_specs=pl.BlockSpec((EXP_TM, D), lambda i, te, na: (i, 0))),
        out_shape=jax.ShapeDtypeStruct((P, D), F32),
        compiler_params=_params(("arbitrary",)),
    )(tile_e, n_active, xs, w_gu, b_gu, w_down, b_down)


def _combine_kernel(alpha, n_tok, dest_ref, ys_hbm, x1_ref, gate_ref, lg_ref, lb_ref, o_ref, rows_ref, sems):
    tm = x1_ref.shape[0]
    i = pl.program_id(0)
    buf = i % 2

    def gather(tile, b):
        base = tile * tm

        def issue(t, carry):
            for k in range(TOP_K):
                _row_copy(ys_hbm, dest_ref[k * n_tok + base + t], rows_ref.at[b, k], t, sems.at[b]).start()
            return carry

        lax.fori_loop(0, tm, issue, 0, unroll=ROW_DMA_UNROLL)

    @pl.when(i == 0)
    def _():
        gather(0, 0)

    @pl.when(i + 1 < pl.num_programs(0))
    def _():
        gather(i + 1, 1 - buf)

    for k in range(TOP_K):
        pltpu.make_async_copy(ys_hbm.at[pl.ds(0, tm)], rows_ref.at[buf, k], sems.at[buf]).wait()

    ff = gate_ref[:, 0:1] * rows_ref[buf, 0]
    for k in range(1, TOP_K):
        ff = ff + gate_ref[:, k:k + 1] * rows_ref[buf, k]
    o_ref[...] = _layer_norm(alpha * x1_ref[...] + ff, lg_ref[...], lb_ref[...])


def _combine(dest_flat, ys, x1, gates_col, ln_g, ln_b, alpha):
    T, D = x1.shape
    tm = min(COMB_TM, T)
    return pl.pallas_call(
        functools.partial(_combine_kernel, alpha, T),
        grid_spec=pltpu.PrefetchScalarGridSpec(
            num_scalar_prefetch=1,
            grid=(T // tm,),
            in_specs=[pl.BlockSpec(memory_space=pl.ANY),
                      pl.BlockSpec((tm, D), lambda i, d: (i, 0)),
                      pl.BlockSpec((tm, TOP_K), lambda i, d: (i, 0)),
                      pl.BlockSpec((1, D), lambda i, d: (0, 0)),
                      pl.BlockSpec((1, D), lambda i, d: (0, 0))],
            out_specs=pl.BlockSpec((tm, D), lambda i, d: (i, 0)),
            scratch_shapes=[pltpu.VMEM((2, TOP_K, tm, D), F32), pltpu.SemaphoreType.DMA((2,))]),
        out_shape=jax.ShapeDtypeStruct((T, D), F32),
        compiler_params=_params(("arbitrary",)),
    )(dest_flat, ys, x1, gates_col, ln_g, ln_b)


def _layer(x2d, B, S, alpha, w_in, b_in, conv_w, conv_b, w_gla_gate, b_gla_gate, m_norm_w, g_norm_w,
           w_branch_m, w_branch_g, w_out, ln1_g, ln1_b, w_router, b_router, w_gu, b_gu, w_down, b_down,
           ln2_g, ln2_b):
    T, D = x2d.shape
    m_end = 2 * M_QKW + 2 * M_VW
    g_beg = m_end + 2 * M_HEADS
    g_end = g_beg + 2 * G_KW + 2 * G_VW
    mg_beg = g_end + G_RANK
    main_cols = [(0, m_end), (g_beg, g_end), (mg_beg, mg_beg + 2 * D)]
    w_main = jnp.concatenate([w_in[:, a:b] for a, b in main_cols], axis=1).astype(BF16)
    b_main = jnp.concatenate([b_in[a:b] for a, b in main_cols])[None, :]
    n_small = 2 * M_HEADS + G_RANK
    w_small = jnp.concatenate([w_in[:, m_end:g_beg], w_in[:, g_end:mg_beg],
                               jnp.zeros((D, SMALL_W - n_small), F32)], axis=1).astype(BF16)
    b_small = jnp.concatenate([b_in[m_end:g_beg], b_in[g_end:mg_beg], jnp.zeros((SMALL_W - n_small,), F32)])[None, :]

    proj, small = _proj(x2d, w_main, b_main, w_small, b_small)
    small_t = small[:, :SUBLANES].T

    ym = _mlstm(proj, small, small_t, conv_w, conv_b[None, :], m_norm_w[None, :], B, S)
    wg_pad = jnp.zeros((SMALL_W, G_KW), F32).at[2 * M_HEADS:n_small].set(w_gla_gate).astype(BF16)
    yg = _gla(proj, small, wg_pad, b_gla_gate[None, :], g_norm_w[None, :], B, S)

    x1, top_e, gates, rank, cnt = _mix(
        ym, yg, proj, x2d, w_branch_m.astype(BF16), w_branch_g.astype(BF16), w_out.astype(BF16),
        ln1_g[None, :], ln1_b[None, :], w_router.T.astype(BF16), b_router[:, None], alpha)

    counts = cnt[:, 0].astype(jnp.int32)
    padded = (counts + EXP_TM - 1) // EXP_TM * EXP_TM
    pend = jnp.cumsum(padded)
    pstart = pend - padded
    n_tiles = (T * TOP_K) // EXP_TM + N_EXPERTS
    tile_start = jnp.arange(n_tiles, dtype=jnp.int32) * EXP_TM
    tile_e = jnp.sum((pend[None, :] <= tile_start[:, None]).astype(jnp.int32), axis=1)
    tile_e = jnp.minimum(tile_e, N_EXPERTS - 1)
    n_active = (pend[-1:] // EXP_TM).astype(jnp.int32)
    dest = rank
    for e in range(N_EXPERTS):
        dest = dest + jnp.where(top_e == e, pstart[e], 0)
    dest = dest.reshape(-1)

    xs = _dispatch(dest, x1, n_tiles * EXP_TM)
    n_rows = n_tiles * EXP_TM
    hole_start = jnp.concatenate([pstart + counts, pend[-1:]])
    hole_cnt = jnp.concatenate([padded - counts, n_rows - pend[-1:]])
    xs = _zero_rows(hole_start, hole_cnt, xs)
    ys = _experts(tile_e, n_active, xs, w_gu.astype(BF16), b_gu[:, None, :], w_down.astype(BF16),
                  b_down[:, None, :])
    return _combine(dest, ys, x1, gates.T, ln2_g[None, :], ln2_b[None, :], alpha)


def kernel(x, w_in, b_in, conv_w, conv_b, w_gla_gate, b_gla_gate, m_norm_w, g_norm_w, w_branch_m, w_branch_g, w_out, ln1_g, ln1_b, w_router, b_router, w_gu, b_gu, w_down, b_down, ln2_g, ln2_b):
    B, S, D = x.shape
    depth = w_in.shape[0]
    alpha = (2 * depth) ** 0.25
    x2d = x.reshape(B * S, D)
    for l in range(depth):
        x2d = _layer(x2d, B, S, alpha, w_in[l], b_in[l], conv_w[l], conv_b[l], w_gla_gate[l], b_gla_gate[l],
                     m_norm_w[l], g_norm_w[l], w_branch_m[l], w_branch_g[l], w_out[l], ln1_g[l], ln1_b[l],
                     w_router[l], b_router[l], w_gu[l], b_gu[l], w_down[l], b_down[l], ln2_g[l], ln2_b[l])
    return x2d.reshape(B, S, D)
```

```python
import functools

import jax
import jax.numpy as jnp
from jax import lax
from jax.experimental import pallas as pl
from jax.experimental.pallas import tpu as pltpu

F32 = jnp.float32
BF16 = jnp.bfloat16

M_HEADS, M_QK, M_V, CONV_W = 4, 128, 256, 4
G_HEADS, G_K, G_V, G_RANK, G_TAU = 4, 128, 256, 16, 16.0
N_EXPERTS, TOP_K = 32, 4
SWIGLU_ALPHA, SWIGLU_LIMIT = 1.702, 7.0
NORM_EPS = 1e-5
M_QKW, M_VW = M_HEADS * M_QK, M_HEADS * M_V
G_KW, G_VW = G_HEADS * G_K, G_HEADS * G_V

LANES = 128
SUBLANES = 8
VMEM_LIMIT = 56 * 1024 * 1024

PROJ_TM, PROJ_TN = 1024, 1024
SMALL_W = LANES
M_CHUNK = 256
G_STEP = 256
G_CHUNK = 64
MIX_TM = 256
EXP_TM = 512
EXP_TF = 1024
EXP_WROWS = 256
DISP_TM = 512
COMB_TM = 256
ROW_DMA_UNROLL = 8
ZERO_CHUNK = 64


def _dot(a, b):
    return jnp.dot(a, b, preferred_element_type=F32)


def _dot_nt(a, b):
    return lax.dot_general(a, b, (((1,), (1,)), ((), ())), preferred_element_type=F32)


def _dot_tn(a, b):
    return lax.dot_general(a, b, (((0,), (0,)), ((), ())), preferred_element_type=F32)


def _split3(x):
    hi = x.astype(BF16)
    r = x - hi.astype(F32)
    mid = r.astype(BF16)
    lo = (r - mid.astype(F32)).astype(BF16)
    return hi, mid, lo


def _sigmoid(x):
    return 1.0 / (1.0 + jnp.exp(-x))


def _log_sigmoid(x):
    return jnp.minimum(x, 0.0) - jnp.log1p(jnp.exp(-jnp.abs(x)))


def _params(sem):
    return pltpu.CompilerParams(dimension_semantics=sem, vmem_limit_bytes=VMEM_LIMIT)


def _proj_kernel(x_ref, w_ref, b_ref, ws_ref, bs_ref, o_ref, os_ref, xb_ref):
    @pl.when(pl.program_id(1) == 0)
    def _():
        xb_ref[...] = x_ref[...].astype(BF16)
        os_ref[...] = _dot(xb_ref[...], ws_ref[...]) + bs_ref[...]

    o_ref[...] = _dot(xb_ref[...], w_ref[...]) + b_ref[...]


def _proj(x, w_bf, b, ws_bf, bs):
    T, K = x.shape
    N = w_bf.shape[1]
    tm = min(PROJ_TM, T)
    tn = PROJ_TN
    return pl.pallas_call(
        _proj_kernel,
        grid=(T // tm, N // tn),
        in_specs=[pl.BlockSpec((tm, K), lambda i, j: (i, 0)),
                  pl.BlockSpec((K, tn), lambda i, j: (0, j)),
                  pl.BlockSpec((1, tn), lambda i, j: (0, j)),
                  pl.BlockSpec((K, SMALL_W), lambda i, j: (0, 0)),
                  pl.BlockSpec((1, SMALL_W), lambda i, j: (0, 0))],
        out_specs=[pl.BlockSpec((tm, tn), lambda i, j: (i, j)),
                   pl.BlockSpec((tm, SMALL_W), lambda i, j: (i, 0))],
        out_shape=[jax.ShapeDtypeStruct((T, N), F32), jax.ShapeDtypeStruct((T, SMALL_W), F32)],
        scratch_shapes=[pltpu.VMEM((tm, K), BF16)],
        compiler_params=_params(("parallel", "arbitrary")),
    )(x, w_bf, b, ws_bf, bs)


def _mlstm_kernel(qk_ref, v_ref, og_ref, sm_ref, smt_ref, cw_ref, cb_ref, nw_ref, o_ref,
                  ext_ref, c_ref, n_ref, m_ref):
    L = M_CHUNK
    c = pl.program_id(1)

    @pl.when(c == 0)
    def _():
        ext_ref[0:SUBLANES, :] = jnp.zeros((SUBLANES, 2 * M_QKW), F32)
        c_ref[...] = jnp.zeros_like(c_ref)
        n_ref[...] = jnp.zeros_like(n_ref)
        m_ref[...] = jnp.zeros_like(m_ref)

    ext_ref[SUBLANES:, :] = qk_ref[...]
    acc = cb_ref[...] + cw_ref[CONV_W - 1:CONV_W, :] * ext_ref[SUBLANES:, :]
    for j in range(CONV_W - 1):
        off = SUBLANES - (CONV_W - 1) + j
        acc = acc + cw_ref[j:j + 1, :] * ext_ref[off:off + L, :]
    ext_ref[0:SUBLANES, :] = ext_ref[L:L + SUBLANES, :]
    qk = acc * _sigmoid(acc)

    row = lax.broadcasted_iota(jnp.int32, (L, L), 0)
    col = lax.broadcasted_iota(jnp.int32, (L, L), 1)
    causal = row >= col
    tri = jnp.where(causal, 1.0, 0.0).astype(BF16)
    triu = jnp.where(col >= row, 1.0, 0.0).astype(BF16)

    sm = sm_ref[...]
    smt = smt_ref[...]
    h1, h2, h3 = _split3(_log_sigmoid(sm))
    bcol_all = _dot(tri, h1) + _dot(tri, h2) + _dot(tri, h3)
    r1, r2, r3 = _split3(_log_sigmoid(smt))
    brow_all = _dot(r1, triu) + _dot(r2, triu) + _dot(r3, triu)

    for h in range(M_HEADS):
        q = qk[:, h * M_QK:(h + 1) * M_QK]
        k = qk[:, M_QKW + h * M_QK:M_QKW + (h + 1) * M_QK] * (M_QK ** -0.5)
        v = v_ref[:, h * M_V:(h + 1) * M_V]
        qb, kb, vb = q.astype(BF16), k.astype(BF16), v.astype(BF16)
        li_col = sm[:, h:h + 1]
        b_col = bcol_all[:, M_HEADS + h:M_HEADS + h + 1]
        b_end = b_col[L - 1:L, :]
        li_row = smt[h:h + 1, :]
        b_row = brow_all[M_HEADS + h:M_HEADS + h + 1, :]
        m_prev = m_ref[h:h + 1, 0:1]
        n_prev = n_ref[h:h + 1, :]
        c_prev = c_ref[h]

        d_log = jnp.where(causal, b_col + (li_row - b_row), -jnp.inf)
        m_inter = b_col + m_prev
        m_t = jnp.maximum(m_inter, jnp.max(d_log, axis=-1, keepdims=True))
        dec = jnp.exp(m_inter - m_t)
        s = _dot_nt(qb, kb) * jnp.exp(d_log - m_t)
        num = _dot(s.astype(BF16), vb) + dec * _dot(qb, c_prev.astype(BF16))
        den = jnp.sum(s, axis=-1, keepdims=True) + dec * jnp.sum(q * n_prev, axis=-1, keepdims=True)
        hh = num / jnp.maximum(jnp.abs(den), jnp.exp(-m_t))

        w_end = b_end - b_col + li_col
        g_end = jnp.max(w_end, axis=0, keepdims=True)
        ke = k * jnp.exp(w_end - g_end)
        d_c = _dot_tn(ke.astype(BF16), vb)
        d_n = jnp.sum(ke, axis=0, keepdims=True)
        m_new = jnp.maximum(b_end + m_prev, g_end)
        a = jnp.exp(b_end + m_prev - m_new)
        cc = jnp.exp(g_end - m_new)
        c_ref[h] = a * c_prev + cc * d_c
        n_ref[h:h + 1, :] = a * n_prev + cc * d_n
        m_ref[h:h + 1, :] = jnp.broadcast_to(m_new, (1, LANES))

        mu = jnp.mean(hh, axis=-1, keepdims=True)
        xc = hh - mu
        var = jnp.mean(xc * xc, axis=-1, keepdims=True)
        sl = slice(h * M_V, (h + 1) * M_V)
        y = xc * lax.rsqrt(var + NORM_EPS) * nw_ref[:, sl] * _sigmoid(og_ref[:, sl])
        o_ref[:, sl] = y.astype(o_ref.dtype)


def _mlstm(proj, small, small_t, conv_w, conv_b, norm_w, B, S):
    L = M_CHUNK
    NC = S // L
    T = B * S
    rowblk = lambda b, c: (b * NC + c, 0)
    const = lambda b, c: (0, 0)
    return pl.pallas_call(
        _mlstm_kernel,
        grid=(B, NC),
        in_specs=[pl.BlockSpec((L, 2 * M_QKW), rowblk),
                  pl.BlockSpec((L, M_VW), lambda b, c: (b * NC + c, 1)),
                  pl.BlockSpec((L, M_VW), lambda b, c: (b * NC + c, 2)),
                  pl.BlockSpec((L, SMALL_W), rowblk),
                  pl.BlockSpec((SUBLANES, L), lambda b, c: (0, b * NC + c)),
                  pl.BlockSpec((CONV_W, 2 * M_QKW), const),
                  pl.BlockSpec((1, 2 * M_QKW), const),
                  pl.BlockSpec((1, M_VW), const)],
        out_specs=pl.BlockSpec((L, M_VW), rowblk),
        out_shape=jax.ShapeDtypeStruct((T, M_VW), BF16),
        scratch_shapes=[pltpu.VMEM((L + SUBLANES, 2 * M_QKW), F32),
                        pltpu.VMEM((M_HEADS, M_QK, M_V), F32),
                        pltpu.VMEM((SUBLANES, M_QK), F32),
                        pltpu.VMEM((SUBLANES, LANES), F32)],
        compiler_params=_params(("parallel", "arbitrary")),
    )(proj, proj, proj, small, small_t, conv_w, conv_b, norm_w)


def _gla_kernel(qk_ref, v_ref, gg_ref, sm_ref, wg_ref, bg_ref, nw_ref, o_ref, st_ref):
    LC = G_CHUNK
    c = pl.program_id(1)

    @pl.when(c == 0)
    def _():
        st_ref[...] = jnp.zeros_like(st_ref)

    pre = _dot(sm_ref[...].astype(BF16), wg_ref[...]) + bg_ref[...]
    log_a = _log_sigmoid(pre) / G_TAU

    row = lax.broadcasted_iota(jnp.int32, (LC, LC), 0)
    col = lax.broadcasted_iota(jnp.int32, (LC, LC), 1)
    causal = row >= col
    tri = jnp.where(causal, 1.0, 0.0).astype(BF16)

    for j in range(G_STEP // LC):
        rs = slice(j * LC, (j + 1) * LC)
        a1, a2, a3 = _split3(log_a[rs, :])
        cum = _dot(tri, a1) + _dot(tri, a2) + _dot(tri, a3)
        cum_end = cum[LC - 1:LC, :]
        e_q = jnp.exp(cum)
        e_k = jnp.exp(-cum)
        e_end = jnp.exp(cum_end - cum)
        e_dec = jnp.exp(cum_end)
        for h in range(G_HEADS):
            ks = slice(h * G_K, (h + 1) * G_K)
            vs = slice(h * G_V, (h + 1) * G_V)
            q = qk_ref[rs, ks] * (G_K ** -0.5)
            k = qk_ref[rs, G_KW + h * G_K:G_KW + (h + 1) * G_K]
            vb = v_ref[rs, vs].astype(BF16)
            q_in = (q * e_q[:, ks]).astype(BF16)
            k_in = (k * e_k[:, ks]).astype(BF16)
            k_end = (k * e_end[:, ks]).astype(BF16)
            st = st_ref[h]
            s = jnp.where(causal, _dot_nt(q_in, k_in), 0.0)
            o = _dot(s.astype(BF16), vb) + _dot_nt(q_in, st.astype(BF16))
            st_ref[h] = st * e_dec[:, ks] + _dot_tn(vb, k_end)
            ms = jnp.mean(o * o, axis=-1, keepdims=True)
            g = gg_ref[rs, vs]
            y = o * lax.rsqrt(ms + NORM_EPS) * nw_ref[:, vs] * (g * _sigmoid(g))
            o_ref[rs, vs] = y.astype(o_ref.dtype)


def _gla(proj, small, wg_pad, bg, norm_w, B, S):
    L = G_STEP
    NC = S // L
    T = B * S
    rowblk = lambda b, c: (b * NC + c, 0)
    const = lambda b, c: (0, 0)
    return pl.pallas_call(
        _gla_kernel,
        grid=(B, NC),
        in_specs=[pl.BlockSpec((L, 2 * G_KW), lambda b, c: (b * NC + c, 3)),
                  pl.BlockSpec((L, G_VW), lambda b, c: (b * NC + c, 4)),
                  pl.BlockSpec((L, G_VW), lambda b, c: (b * NC + c, 5)),
                  pl.BlockSpec((L, SMALL_W), rowblk),
                  pl.BlockSpec((SMALL_W, G_KW), const),
                  pl.BlockSpec((1, G_KW), const),
                  pl.BlockSpec((1, G_VW), const)],
        out_specs=pl.BlockSpec((L, G_VW), rowblk),
        out_shape=jax.ShapeDtypeStruct((T, G_VW), BF16),
        scratch_shapes=[pltpu.VMEM((G_HEADS, G_V, G_K), F32)],
        compiler_params=_params(("parallel", "arbitrary")),
    )(proj, proj, proj, small, wg_pad, bg, norm_w)


def _layer_norm(r, g, b):
    mu = jnp.mean(r, axis=-1, keepdims=True)
    xc = r - mu
    var = jnp.mean(xc * xc, axis=-1, keepdims=True)
    return xc * lax.rsqrt(var + NORM_EPS) * g + b


def _mix_kernel(alpha, ym_ref, yg_ref, gm_ref, gg_ref, x_ref, wbm_ref, wbg_ref, wout_ref, lg_ref, lb_ref,
                wr_ref, br_ref, x1_ref, tope_ref, gate_ref, rank_ref, cnt_ref, carry_ref):
    tm = x_ref.shape[0]
    i = pl.program_id(0)

    @pl.when(i == 0)
    def _():
        carry_ref[...] = jnp.zeros_like(carry_ref)

    pm = _dot(ym_ref[...], wbm_ref[...])
    pg = _dot(yg_ref[...], wbg_ref[...])
    z = _sigmoid(gm_ref[...]) * pm + _sigmoid(gg_ref[...]) * pg
    mix = _dot(z.astype(BF16), wout_ref[...])
    x1 = _layer_norm(alpha * x_ref[...] + mix, lg_ref[...], lb_ref[...])
    x1_ref[...] = x1

    logits = _dot_nt(wr_ref[...], x1.astype(BF16)) + br_ref[...]
    eidx = lax.broadcasted_iota(jnp.int32, (N_EXPERTS, tm), 0)
    vals, hots = [], []
    cur = logits
    for k in range(TOP_K):
        mx = jnp.max(cur, axis=0, keepdims=True)
        idx = jnp.min(jnp.where(cur == mx, eidx, N_EXPERTS), axis=0, keepdims=True)
        hot = eidx == idx
        cur = jnp.where(hot, -jnp.inf, cur)
        vals.append(mx)
        hots.append(hot)
        tope_ref[k:k + 1, :] = idx
    exps = [jnp.exp(v - vals[0]) for v in vals]
    tot = exps[0] + exps[1] + exps[2] + exps[3]
    for k in range(TOP_K):
        gate_ref[k:k + 1, :] = exps[k] / tot

    sel = jnp.zeros((N_EXPERTS, tm), F32)
    for hot in hots:
        sel = sel + jnp.where(hot, 1.0, 0.0)
    row = lax.broadcasted_iota(jnp.int32, (tm, tm), 0)
    col = lax.broadcasted_iota(jnp.int32, (tm, tm), 1)
    before = jnp.where(row < col, 1.0, 0.0).astype(BF16)
    pos = _dot(sel.astype(BF16), before) + carry_ref[:, 0:1]
    for k in range(TOP_K):
        rk = jnp.sum(jnp.where(hots[k], pos, 0.0), axis=0, keepdims=True)
        rank_ref[k:k + 1, :] = rk.astype(jnp.int32)
    carry_ref[...] = carry_ref[...] + jnp.sum(sel, axis=1, keepdims=True)
    cnt_ref[...] = carry_ref[...]


def _mix(ym, yg, proj, x2d, wbm, wbg, wout, ln_g, ln_b, wr_t, br, alpha):
    T, D = x2d.shape
    tm = min(MIX_TM, T)
    rowblk = lambda i: (i, 0)
    const = lambda i: (0, 0)
    once = dict(pipeline_mode=pl.Buffered(1))
    return pl.pallas_call(
        functools.partial(_mix_kernel, alpha),
        grid=(T // tm,),
        in_specs=[pl.BlockSpec((tm, M_VW), rowblk),
                  pl.BlockSpec((tm, G_VW), rowblk),
                  pl.BlockSpec((tm, D), lambda i: (i, 3)),
                  pl.BlockSpec((tm, D), lambda i: (i, 4)),
                  pl.BlockSpec((tm, D), rowblk),
                  pl.BlockSpec((M_VW, D), const, **once),
                  pl.BlockSpec((G_VW, D), const, **once),
                  pl.BlockSpec((D, D), const, **once),
                  pl.BlockSpec((1, D), const),
                  pl.BlockSpec((1, D), const),
                  pl.BlockSpec((N_EXPERTS, D), const),
                  pl.BlockSpec((N_EXPERTS, 1), const)],
        out_specs=[pl.BlockSpec((tm, D), rowblk),
                   pl.BlockSpec((TOP_K, tm), lambda i: (0, i)),
                   pl.BlockSpec((TOP_K, tm), lambda i: (0, i)),
                   pl.BlockSpec((TOP_K, tm), lambda i: (0, i)),
                   pl.BlockSpec((N_EXPERTS, LANES), const)],
        out_shape=[jax.ShapeDtypeStruct((T, D), F32),
                   jax.ShapeDtypeStruct((TOP_K, T), jnp.int32),
                   jax.ShapeDtypeStruct((TOP_K, T), F32),
                   jax.ShapeDtypeStruct((TOP_K, T), jnp.int32),
                   jax.ShapeDtypeStruct((N_EXPERTS, LANES), F32)],
        scratch_shapes=[pltpu.VMEM((N_EXPERTS, LANES), F32)],
        compiler_params=_params(("arbitrary",)),
    )(ym, yg, proj, proj, x2d, wbm, wbg, wout, ln_g, ln_b, wr_t, br)


def _row_copy(src, src_row, dst, dst_row, sem):
    return pltpu.make_async_copy(src.at[pl.ds(src_row, 1)], dst.at[pl.ds(dst_row, 1)], sem)


def _dispatch_kernel(n_tok, dest_ref, x1_ref, xs_hbm, sem):
    tm = x1_ref.shape[0]
    base = pl.program_id(0) * tm

    def issue(t, carry):
        for k in range(TOP_K):
            _row_copy(x1_ref, t, xs_hbm, dest_ref[k * n_tok + base + t], sem).start()
        return carry

    lax.fori_loop(0, tm, issue, 0, unroll=ROW_DMA_UNROLL)
    for k in range(TOP_K):
        pltpu.make_async_copy(x1_ref, xs_hbm.at[pl.ds(0, tm)], sem).wait()


def _dispatch(dest_flat, x1, n_rows):
    T, D = x1.shape
    tm = min(DISP_TM, T)
    return pl.pallas_call(
        functools.partial(_dispatch_kernel, T),
        grid_spec=pltpu.PrefetchScalarGridSpec(
            num_scalar_prefetch=1,
            grid=(T // tm,),
            in_specs=[pl.BlockSpec((tm, D), lambda i, d: (i, 0))],
            out_specs=pl.BlockSpec(memory_space=pl.ANY),
            scratch_shapes=[pltpu.SemaphoreType.DMA(())]),
        out_shape=jax.ShapeDtypeStruct((n_rows, D), x1.dtype),
        compiler_params=pltpu.CompilerParams(dimension_semantics=("arbitrary",), has_side_effects=True),
    )(dest_flat, x1)


def _zero_rows_kernel(start_ref, cnt_ref, xs_in_hbm, xs_hbm, zrow_ref, sem):
    del xs_in_hbm
    zrow_ref[...] = jnp.zeros_like(zrow_ref)

    def per_range(e, carry):
        start, cnt = start_ref[e], cnt_ref[e]
        head = jnp.minimum(cnt, (-start) & (SUBLANES - 1))
        mid = start + head
        n_blk = (cnt - head) // ZERO_CHUNK
        tail = mid + n_blk * ZERO_CHUNK
        n_tail = cnt - head - n_blk * ZERO_CHUNK

        def blk_copy(r):
            dst = pl.multiple_of(mid + r * ZERO_CHUNK, SUBLANES)
            return pltpu.make_async_copy(zrow_ref, xs_hbm.at[pl.ds(dst, ZERO_CHUNK)], sem)

        def rows(first, n):
            def issue(r, c):
                _row_copy(zrow_ref, 0, xs_hbm, first + r, sem).start()
                return c

            def drain(r, c):
                _row_copy(zrow_ref, 0, xs_hbm, 0, sem).wait()
                return c

            lax.fori_loop(0, n, issue, 0)
            lax.fori_loop(0, n, drain, 0)

        def issue_blk(r, c):
            blk_copy(r).start()
            return c

        def drain_blk(r, c):
            blk_copy(0).wait()
            return c

        rows(start, head)
        lax.fori_loop(0, n_blk, issue_blk, 0)
        lax.fori_loop(0, n_blk, drain_blk, 0)
        rows(tail, n_tail)
        return carry

    lax.fori_loop(0, start_ref.shape[0], per_range, 0)


def _zero_rows(start, cnt, xs):
    return pl.pallas_call(
        _zero_rows_kernel,
        grid_spec=pltpu.PrefetchScalarGridSpec(
            num_scalar_prefetch=2,
            grid=(1,),
            in_specs=[pl.BlockSpec(memory_space=pl.ANY)],
            out_specs=pl.BlockSpec(memory_space=pl.ANY),
            scratch_shapes=[pltpu.VMEM((ZERO_CHUNK, xs.shape[1]), xs.dtype), pltpu.SemaphoreType.DMA(())]),
        out_shape=jax.ShapeDtypeStruct(xs.shape, xs.dtype),
        input_output_aliases={2: 0},
        compiler_params=pltpu.CompilerParams(dimension_semantics=("arbitrary",), has_side_effects=True),
    )(start, cnt, xs)


def _expert_kernel(te_ref, na_ref, xs_ref, wgu_hbm, bgu_ref, wd_hbm, bd_ref, o_ref, wgu_ref, wd_ref, stage_ref, sems):
    i = pl.program_id(0)
    fh, d = wd_ref.shape
    tf = min(EXP_TF, fh)
    rows = stage_ref.shape[1]
    n_gu, n_d = wgu_ref.shape[0] // rows, fh // rows
    active = i < na_ref[0]
    e = te_ref[i]
    changed = jnp.logical_or(i == 0, e != te_ref[jnp.maximum(i - 1, 0)])

    @pl.when(jnp.logical_and(active, changed))
    def _():
        def chunk(c):
            b = c % 2
            if c < n_gu:
                return pltpu.make_async_copy(wgu_hbm.at[e, pl.ds(c * rows, rows), :], stage_ref.at[b], sems.at[b])
            r0 = (c - n_gu) * rows
            return pltpu.make_async_copy(wd_hbm.at[e, pl.ds(r0, rows), :], stage_ref.at[b, :, 0:d], sems.at[b])

        chunk(0).start()
        for c in range(n_gu + n_d):
            if c + 1 < n_gu + n_d:
                chunk(c + 1).start()
            chunk(c).wait()
            if c < n_gu:
                wgu_ref[c * rows:(c + 1) * rows, :] = stage_ref[c % 2].astype(BF16)
            else:
                r0 = (c - n_gu) * rows
                wd_ref[r0:r0 + rows, :] = stage_ref[c % 2, :, 0:d].astype(BF16)

    @pl.when(active)
    def _():
        a = xs_ref[...].astype(BF16)
        acc = None
        for f in range(fh // tf):
            glu = slice(f * tf, (f + 1) * tf)
            lin = slice(fh + f * tf, fh + (f + 1) * tf)
            g = _dot(a, wgu_ref[:, glu]) + bgu_ref[:, glu]
            l = _dot(a, wgu_ref[:, lin]) + bgu_ref[:, lin]
            a_glu = jnp.minimum(g, SWIGLU_LIMIT)
            a_lin = jnp.clip(l, -SWIGLU_LIMIT, SWIGLU_LIMIT)
            act = a_glu * _sigmoid(SWIGLU_ALPHA * a_glu) * (a_lin + 1.0)
            part = _dot(act.astype(BF16), wd_ref[glu, :])
            acc = part if acc is None else acc + part
        o_ref[...] = acc + bd_ref[...]

    @pl.when(i >= na_ref[0])
    def _():
        o_ref[...] = jnp.zeros_like(o_ref)


def _experts(tile_e, n_active, xs, w_gu, b_gu, w_down, b_down):
    P, D = xs.shape
    E, _, F2 = w_gu.shape
    Fh = F2 // 2
    n_tiles = P // EXP_TM

    def tile(i, na):
        return jnp.minimum(i, na[0] - 1)

    return pl.pallas_call(
        _expert_kernel,
        grid_spec=pltpu.PrefetchScalarGridSpec(
            num_scalar_prefetch=2,
            grid=(n_tiles,),
            in_specs=[
                pl.BlockSpec((EXP_TM, D), lambda i, te, na: (tile(i, na), 0)),
                pl.BlockSpec(memory_space=pl.ANY),
                pl.BlockSpec((None, 1, F2), lambda i, te, na: (te[tile(i, na)], 0, 0)),
                pl.BlockSpec(memory_space=pl.ANY),
                pl.BlockSpec((None, 1, D), lambda i, te, na: (te[tile(i, na)], 0, 0)),
            ],
            out_specs=pl.BlockSpec((EXP_TM, D), lambda i, te, na: (i, 0)),
            scratch_shapes=[pltpu.VMEM((D, F2), BF16),
                            pltpu.VMEM((Fh, D), BF16),
                            pltpu.VMEM((2, EXP_WROWS, F2), F32),
                            pltpu.SemaphoreType.DMA((2,))]),
        out_shape=jax.ShapeDtypeStruct((P, D), F32),
        compiler_params=_params(("arbitrary",)),
    )(tile_e, n_active, xs, w_gu, b_gu, w_down, b_down)


def _combine_kernel(alpha, n_tok, dest_ref, ys_hbm, x1_ref, gate_ref, lg_ref, lb_ref, o_ref, rows_ref, sems):
    tm = x1_ref.shape[0]
    i = pl.program_id(0)
    buf = i % 2

    def gather(tile, b):
        base = tile * tm

        def issue(t, carry):
            for k in range(TOP_K):
                _row_copy(ys_hbm, dest_ref[k * n_tok + base + t], rows_ref.at[b, k], t, sems.at[b]).start()
            return carry

        lax.fori_loop(0, tm, issue, 0, unroll=ROW_DMA_UNROLL)

    @pl.when(i == 0)
    def _():
        gather(0, 0)

    @pl.when(i + 1 < pl.num_programs(0))
    def _():
        gather(i + 1, 1 - buf)

    for k in range(TOP_K):
        pltpu.make_async_copy(ys_hbm.at[pl.ds(0, tm)], rows_ref.at[buf, k], sems.at[buf]).wait()

    ff = gate_ref[:, 0:1] * rows_ref[buf, 0]
    for k in range(1, TOP_K):
        ff = ff + gate_ref[:, k:k + 1] * rows_ref[buf, k]
    o_ref[...] = _layer_norm(alpha * x1_ref[...] + ff, lg_ref[...], lb_ref[...])


def _combine(dest_flat, ys, x1, gates_col, ln_g, ln_b, alpha):
    T, D = x1.shape
    tm = min(COMB_TM, T)
    return pl.pallas_call(
        functools.partial(_combine_kernel, alpha, T),
        grid_spec=pltpu.PrefetchScalarGridSpec(
            num_scalar_prefetch=1,
            grid=(T // tm,),
            in_specs=[pl.BlockSpec(memory_space=pl.ANY),
                      pl.BlockSpec((tm, D), lambda i, d: (i, 0)),
                      pl.BlockSpec((tm, TOP_K), lambda i, d: (i, 0)),
                      pl.BlockSpec((1, D), lambda i, d: (0, 0)),
                      pl.BlockSpec((1, D), lambda i, d: (0, 0))],
            out_specs=pl.BlockSpec((tm, D), lambda i, d: (i, 0)),
            scratch_shapes=[pltpu.VMEM((2, TOP_K, tm, D), F32), pltpu.SemaphoreType.DMA((2,))]),
        out_shape=jax.ShapeDtypeStruct((T, D), F32),
        compiler_params=_params(("arbitrary",)),
    )(dest_flat, ys, x1, gates_col, ln_g, ln_b)


def _layer(x2d, B, S, alpha, w_in, b_in, conv_w, conv_b, w_gla_gate, b_gla_gate, m_norm_w, g_norm_w,
           w_branch_m, w_branch_g, w_out, ln1_g, ln1_b, w_router, b_router, w_gu, b_gu, w_down, b_down,
           ln2_g, ln2_b):
    T, D = x2d.shape
    m_end = 2 * M_QKW + 2 * M_VW
    g_beg = m_end + 2 * M_HEADS
    g_end = g_beg + 2 * G_KW + 2 * G_VW
    mg_beg = g_end + G_RANK
    main_cols = [(0, m_end), (g_beg, g_end), (mg_beg, mg_beg + 2 * D)]
    w_main = jnp.concatenate([w_in[:, a:b] for a, b in main_cols], axis=1).astype(BF16)
    b_main = jnp.concatenate([b_in[a:b] for a, b in main_cols])[None, :]
    n_small = 2 * M_HEADS + G_RANK
    w_small = jnp.concatenate([w_in[:, m_end:g_beg], w_in[:, g_end:mg_beg],
                               jnp.zeros((D, SMALL_W - n_small), F32)], axis=1).astype(BF16)
    b_small = jnp.concatenate([b_in[m_end:g_beg], b_in[g_end:mg_beg], jnp.zeros((SMALL_W - n_small,), F32)])[None, :]

    proj, small = _proj(x2d, w_main, b_main, w_small, b_small)
    small_t = small[:, :SUBLANES].T

    ym = _mlstm(proj, small, small_t, conv_w, conv_b[None, :], m_norm_w[None, :], B, S)
    wg_pad = jnp.zeros((SMALL_W, G_KW), F32).at[2 * M_HEADS:n_small].set(w_gla_gate).astype(BF16)
    yg = _gla(proj, small, wg_pad, b_gla_gate[None, :], g_norm_w[None, :], B, S)

    x1, top_e, gates, rank, cnt = _mix(
        ym, yg, proj, x2d, w_branch_m.astype(BF16), w_branch_g.astype(BF16), w_out.astype(BF16),
        ln1_g[None, :], ln1_b[None, :], w_router.T.astype(BF16), b_router[:, None], alpha)

    counts = cnt[:, 0].astype(jnp.int32)
    padded = (counts + EXP_TM - 1) // EXP_TM * EXP_TM
    pend = jnp.cumsum(padded)
    pstart = pend - padded
    n_tiles = (T * TOP_K) // EXP_TM + N_EXPERTS
    tile_start = jnp.arange(n_tiles, dtype=jnp.int32) * EXP_TM
    tile_e = jnp.sum((pend[None, :] <= tile_start[:, None]).astype(jnp.int32), axis=1)
    tile_e = jnp.minimum(tile_e, N_EXPERTS - 1)
    n_active = (pend[-1:] // EXP_TM).astype(jnp.int32)
    dest = rank
    for e in range(N_EXPERTS):
        dest = dest + jnp.where(top_e == e, pstart[e], 0)
    dest = dest.reshape(-1)

    xs = _dispatch(dest, x1, n_tiles * EXP_TM)
    n_rows = n_tiles * EXP_TM
    hole_start = jnp.concatenate([pstart + counts, pend[-1:]])
    hole_cnt = jnp.concatenate([padded - counts, n_rows - pend[-1:]])
    xs = _zero_rows(hole_start, hole_cnt, xs)
    ys = _experts(tile_e, n_active, xs, w_gu, b_gu[:, None, :], w_down, b_down[:, None, :])
    return _combine(dest, ys, x1, gates.T, ln2_g[None, :], ln2_b[None, :], alpha)


def kernel(x, w_in, b_in, conv_w, conv_b, w_gla_gate, b_gla_gate, m_norm_w, g_norm_w, w_branch_m, w_branch_g, w_out, ln1_g, ln1_b, w_router, b_router, w_gu, b_gu, w_down, b_down, ln2_g, ln2_b):
    B, S, D = x.shape
    depth = w_in.shape[0]
    alpha = (2 * depth) ** 0.25
    x2d = x.reshape(B * S, D)
    for l in range(depth):
        x2d = _layer(x2d, B, S, alpha, w_in[l], b_in[l], conv_w[l], conv_b[l], w_gla_gate[l], b_gla_gate[l],
                     m_norm_w[l], g_norm_w[l], w_branch_m[l], w_branch_g[l], w_out[l], ln1_g[l], ln1_b[l],
                     w_router[l], b_router[l], w_gu[l], b_gu[l], w_down[l], b_down[l], ln2_g[l], ln2_b[l])
    return x2d.reshape(B, S, D)
```

```python
import functools

import jax
import jax.numpy as jnp
from jax import lax
from jax.experimental import pallas as pl
from jax.experimental.pallas import tpu as pltpu

F32 = jnp.float32
BF16 = jnp.bfloat16

M_HEADS, M_QK, M_V, CONV_W = 4, 128, 256, 4
G_HEADS, G_K, G_V, G_RANK, G_TAU = 4, 128, 256, 16, 16.0
N_EXPERTS, TOP_K = 32, 4
SWIGLU_ALPHA, SWIGLU_LIMIT = 1.702, 7.0
NORM_EPS = 1e-5
M_QKW, M_VW = M_HEADS * M_QK, M_HEADS * M_V
G_KW, G_VW = G_HEADS * G_K, G_HEADS * G_V

LANES = 128
SUBLANES = 8
VMEM_LIMIT = 56 * 1024 * 1024
EXP_VMEM_LIMIT = 60 * 1024 * 1024

PROJ_TM, PROJ_TN = 1024, 1024
SMALL_W = LANES
M_CHUNK = 256
G_STEP = 256
G_CHUNK = 64
MIX_TM = 256
EXP_TM = 512
EXP_TF = 1024
EXP_WPIECE = 512
DISP_TM = 512
COMB_TM = 256
ROW_DMA_UNROLL = 8
ZERO_CHUNK = 64


def _dot(a, b):
    return jnp.dot(a, b, preferred_element_type=F32)


def _dot_nt(a, b):
    return lax.dot_general(a, b, (((1,), (1,)), ((), ())), preferred_element_type=F32)


def _dot_tn(a, b):
    return lax.dot_general(a, b, (((0,), (0,)), ((), ())), preferred_element_type=F32)


def _split3(x):
    hi = x.astype(BF16)
    r = x - hi.astype(F32)
    mid = r.astype(BF16)
    lo = (r - mid.astype(F32)).astype(BF16)
    return hi, mid, lo


def _sigmoid(x):
    return 0.5 * jnp.tanh(0.5 * x) + 0.5


def _log_sigmoid(x):
    return jnp.minimum(x, 0.0) - jnp.log1p(jnp.exp(-jnp.abs(x)))


def _params(sem, vmem_limit=VMEM_LIMIT):
    return pltpu.CompilerParams(dimension_semantics=sem, vmem_limit_bytes=vmem_limit)


def _proj_kernel(x_ref, w_ref, b_ref, ws_ref, bs_ref, o_ref, os_ref, xb_ref):
    @pl.when(pl.program_id(1) == 0)
    def _():
        xb_ref[...] = x_ref[...].astype(BF16)
        os_ref[...] = _dot(xb_ref[...], ws_ref[...]) + bs_ref[...]

    o_ref[...] = _dot(xb_ref[...], w_ref[...]) + b_ref[...]


def _proj(x, w_bf, b, ws_bf, bs):
    T, K = x.shape
    N = w_bf.shape[1]
    tm = min(PROJ_TM, T)
    tn = PROJ_TN
    return pl.pallas_call(
        _proj_kernel,
        grid=(T // tm, N // tn),
        in_specs=[pl.BlockSpec((tm, K), lambda i, j: (i, 0)),
                  pl.BlockSpec((K, tn), lambda i, j: (0, j)),
                  pl.BlockSpec((1, tn), lambda i, j: (0, j)),
                  pl.BlockSpec((K, SMALL_W), lambda i, j: (0, 0)),
                  pl.BlockSpec((1, SMALL_W), lambda i, j: (0, 0))],
        out_specs=[pl.BlockSpec((tm, tn), lambda i, j: (i, j)),
                   pl.BlockSpec((tm, SMALL_W), lambda i, j: (i, 0))],
        out_shape=[jax.ShapeDtypeStruct((T, N), F32), jax.ShapeDtypeStruct((T, SMALL_W), F32)],
        scratch_shapes=[pltpu.VMEM((tm, K), BF16)],
        compiler_params=_params(("parallel", "arbitrary")),
    )(x, w_bf, b, ws_bf, bs)


def _mlstm_kernel(qk_ref, v_ref, og_ref, sm_ref, smt_ref, cw_ref, cb_ref, nw_ref, o_ref,
                  ext_ref, c_ref, n_ref, m_ref):
    L = M_CHUNK
    c = pl.program_id(1)

    @pl.when(c == 0)
    def _():
        ext_ref[0:SUBLANES, :] = jnp.zeros((SUBLANES, 2 * M_QKW), F32)
        c_ref[...] = jnp.zeros_like(c_ref)
        n_ref[...] = jnp.zeros_like(n_ref)
        m_ref[...] = jnp.zeros_like(m_ref)

    ext_ref[SUBLANES:, :] = qk_ref[...]
    acc = cb_ref[...] + cw_ref[CONV_W - 1:CONV_W, :] * ext_ref[SUBLANES:, :]
    for j in range(CONV_W - 1):
        off = SUBLANES - (CONV_W - 1) + j
        acc = acc + cw_ref[j:j + 1, :] * ext_ref[off:off + L, :]
    ext_ref[0:SUBLANES, :] = ext_ref[L:L + SUBLANES, :]
    qk = acc * _sigmoid(acc)

    row = lax.broadcasted_iota(jnp.int32, (L, L), 0)
    col = lax.broadcasted_iota(jnp.int32, (L, L), 1)
    causal = row >= col
    tri = jnp.where(causal, 1.0, 0.0).astype(BF16)
    triu = jnp.where(col >= row, 1.0, 0.0).astype(BF16)

    sm = sm_ref[...]
    smt = smt_ref[...]
    h1, h2, h3 = _split3(_log_sigmoid(sm))
    bcol_all = _dot(tri, h1) + _dot(tri, h2) + _dot(tri, h3)
    r1, r2, r3 = _split3(_log_sigmoid(smt))
    brow_all = _dot(r1, triu) + _dot(r2, triu) + _dot(r3, triu)

    for h in range(M_HEADS):
        q = qk[:, h * M_QK:(h + 1) * M_QK]
        k = qk[:, M_QKW + h * M_QK:M_QKW + (h + 1) * M_QK] * (M_QK ** -0.5)
        v = v_ref[:, h * M_V:(h + 1) * M_V]
        qb, kb, vb = q.astype(BF16), k.astype(BF16), v.astype(BF16)
        li_col = sm[:, h:h + 1]
        b_col = bcol_all[:, M_HEADS + h:M_HEADS + h + 1]
        b_end = b_col[L - 1:L, :]
        li_row = smt[h:h + 1, :]
        b_row = brow_all[M_HEADS + h:M_HEADS + h + 1, :]
        m_prev = m_ref[h:h + 1, 0:1]
        n_prev = n_ref[h:h + 1, :]
        c_prev = c_ref[h]

        d_log = jnp.where(causal, b_col + (li_row - b_row), -jnp.inf)
        m_inter = b_col + m_prev
        m_t = jnp.maximum(m_inter, jnp.max(d_log, axis=-1, keepdims=True))
        dec = jnp.exp(m_inter - m_t)
        s = _dot_nt(qb, kb) * jnp.exp(d_log - m_t)
        num = _dot(s.astype(BF16), vb) + dec * _dot(qb, c_prev.astype(BF16))
        den = jnp.sum(s, axis=-1, keepdims=True) + dec * jnp.sum(q * n_prev, axis=-1, keepdims=True)
        hh = num / jnp.maximum(jnp.abs(den), jnp.exp(-m_t))

        w_end = b_end - b_col + li_col
        g_end = jnp.max(w_end, axis=0, keepdims=True)
        ke = k * jnp.exp(w_end - g_end)
        d_c = _dot_tn(ke.astype(BF16), vb)
        d_n = jnp.sum(ke, axis=0, keepdims=True)
        m_new = jnp.maximum(b_end + m_prev, g_end)
        a = jnp.exp(b_end + m_prev - m_new)
        cc = jnp.exp(g_end - m_new)
        c_ref[h] = a * c_prev + cc * d_c
        n_ref[h:h + 1, :] = a * n_prev + cc * d_n
        m_ref[h:h + 1, :] = jnp.broadcast_to(m_new, (1, LANES))

        mu = jnp.mean(hh, axis=-1, keepdims=True)
        xc = hh - mu
        var = jnp.mean(xc * xc, axis=-1, keepdims=True)
        sl = slice(h * M_V, (h + 1) * M_V)
        y = xc * lax.rsqrt(var + NORM_EPS) * nw_ref[:, sl] * _sigmoid(og_ref[:, sl])
        o_ref[:, sl] = y.astype(o_ref.dtype)


def _mlstm(proj, small, small_t, conv_w, conv_b, norm_w, B, S):
    L = M_CHUNK
    NC = S // L
    T = B * S
    rowblk = lambda b, c: (b * NC + c, 0)
    const = lambda b, c: (0, 0)
    return pl.pallas_call(
        _mlstm_kernel,
        grid=(B, NC),
        in_specs=[pl.BlockSpec((L, 2 * M_QKW), rowblk),
                  pl.BlockSpec((L, M_VW), lambda b, c: (b * NC + c, 1)),
                  pl.BlockSpec((L, M_VW), lambda b, c: (b * NC + c, 2)),
                  pl.BlockSpec((L, SMALL_W), rowblk),
                  pl.BlockSpec((SUBLANES, L), lambda b, c: (0, b * NC + c)),
                  pl.BlockSpec((CONV_W, 2 * M_QKW), const),
                  pl.BlockSpec((1, 2 * M_QKW), const),
                  pl.BlockSpec((1, M_VW), const)],
        out_specs=pl.BlockSpec((L, M_VW), rowblk),
        out_shape=jax.ShapeDtypeStruct((T, M_VW), BF16),
        scratch_shapes=[pltpu.VMEM((L + SUBLANES, 2 * M_QKW), F32),
                        pltpu.VMEM((M_HEADS, M_QK, M_V), F32),
                        pltpu.VMEM((SUBLANES, M_QK), F32),
                        pltpu.VMEM((SUBLANES, LANES), F32)],
        compiler_params=_params(("parallel", "arbitrary")),
    )(proj, proj, proj, small, small_t, conv_w, conv_b, norm_w)


def _gla_kernel(qk_ref, v_ref, gg_ref, sm_ref, wg_ref, bg_ref, nw_ref, o_ref, st_ref):
    LC = G_CHUNK
    c = pl.program_id(1)

    @pl.when(c == 0)
    def _():
        st_ref[...] = jnp.zeros_like(st_ref)

    pre = _dot(sm_ref[...].astype(BF16), wg_ref[...]) + bg_ref[...]
    log_a = _log_sigmoid(pre) / G_TAU

    row = lax.broadcasted_iota(jnp.int32, (LC, LC), 0)
    col = lax.broadcasted_iota(jnp.int32, (LC, LC), 1)
    causal = row >= col
    tri = jnp.where(causal, 1.0, 0.0).astype(BF16)

    for j in range(G_STEP // LC):
        rs = slice(j * LC, (j + 1) * LC)
        a1, a2, a3 = _split3(log_a[rs, :])
        cum = _dot(tri, a1) + _dot(tri, a2) + _dot(tri, a3)
        cum_end = cum[LC - 1:LC, :]
        e_q = jnp.exp(cum)
        e_k = jnp.exp(-cum)
        e_end = jnp.exp(cum_end - cum)
        e_dec = jnp.exp(cum_end)
        for h in range(G_HEADS):
            ks = slice(h * G_K, (h + 1) * G_K)
            vs = slice(h * G_V, (h + 1) * G_V)
            q = qk_ref[rs, ks] * (G_K ** -0.5)
            k = qk_ref[rs, G_KW + h * G_K:G_KW + (h + 1) * G_K]
            vb = v_ref[rs, vs].astype(BF16)
            q_in = (q * e_q[:, ks]).astype(BF16)
            k_in = (k * e_k[:, ks]).astype(BF16)
            k_end = (k * e_end[:, ks]).astype(BF16)
            st = st_ref[h]
            s = jnp.where(causal, _dot_nt(q_in, k_in), 0.0)
            o = _dot(s.astype(BF16), vb) + _dot_nt(q_in, st.astype(BF16))
            st_ref[h] = st * e_dec[:, ks] + _dot_tn(vb, k_end)
            ms = jnp.mean(o * o, axis=-1, keepdims=True)
            g = gg_ref[rs, vs]
            y = o * lax.rsqrt(ms + NORM_EPS) * nw_ref[:, vs] * (g * _sigmoid(g))
            o_ref[rs, vs] = y.astype(o_ref.dtype)


def _gla(proj, small, wg_pad, bg, norm_w, B, S):
    L = G_STEP
    NC = S // L
    T = B * S
    rowblk = lambda b, c: (b * NC + c, 0)
    const = lambda b, c: (0, 0)
    return pl.pallas_call(
        _gla_kernel,
        grid=(B, NC),
        in_specs=[pl.BlockSpec((L, 2 * G_KW), lambda b, c: (b * NC + c, 3)),
                  pl.BlockSpec((L, G_VW), lambda b, c: (b * NC + c, 4)),
                  pl.BlockSpec((L, G_VW), lambda b, c: (b * NC + c, 5)),
                  pl.BlockSpec((L, SMALL_W), rowblk),
                  pl.BlockSpec((SMALL_W, G_KW), const),
                  pl.BlockSpec((1, G_KW), const),
                  pl.BlockSpec((1, G_VW), const)],
        out_specs=pl.BlockSpec((L, G_VW), rowblk),
        out_shape=jax.ShapeDtypeStruct((T, G_VW), BF16),
        scratch_shapes=[pltpu.VMEM((G_HEADS, G_V, G_K), F32)],
        compiler_params=_params(("parallel", "arbitrary")),
    )(proj, proj, proj, small, wg_pad, bg, norm_w)


def _layer_norm(r, g, b):
    mu = jnp.mean(r, axis=-1, keepdims=True)
    xc = r - mu
    var = jnp.mean(xc * xc, axis=-1, keepdims=True)
    return xc * lax.rsqrt(var + NORM_EPS) * g + b


def _mix_kernel(alpha, ym_ref, yg_ref, gm_ref, gg_ref, x_ref, wbm_ref, wbg_ref, wout_ref, lg_ref, lb_ref,
                wr_ref, br_ref, x1_ref, tope_ref, gate_ref, rank_ref, cnt_ref, carry_ref):
    tm = x_ref.shape[0]
    i = pl.program_id(0)

    @pl.when(i == 0)
    def _():
        carry_ref[...] = jnp.zeros_like(carry_ref)

    pm = _dot(ym_ref[...], wbm_ref[...])
    pg = _dot(yg_ref[...], wbg_ref[...])
    z = _sigmoid(gm_ref[...]) * pm + _sigmoid(gg_ref[...]) * pg
    mix = _dot(z.astype(BF16), wout_ref[...])
    x1 = _layer_norm(alpha * x_ref[...] + mix, lg_ref[...], lb_ref[...])
    x1_ref[...] = x1

    logits = _dot_nt(wr_ref[...], x1.astype(BF16)) + br_ref[...]
    eidx = lax.broadcasted_iota(jnp.int32, (N_EXPERTS, tm), 0)
    vals, hots = [], []
    cur = logits
    for k in range(TOP_K):
        mx = jnp.max(cur, axis=0, keepdims=True)
        idx = jnp.min(jnp.where(cur == mx, eidx, N_EXPERTS), axis=0, keepdims=True)
        hot = eidx == idx
        cur = jnp.where(hot, -jnp.inf, cur)
        vals.append(mx)
        hots.append(hot)
        tope_ref[k:k + 1, :] = idx
    exps = [jnp.exp(v - vals[0]) for v in vals]
    tot = exps[0] + exps[1] + exps[2] + exps[3]
    for k in range(TOP_K):
        gate_ref[k:k + 1, :] = exps[k] / tot

    sel = jnp.zeros((N_EXPERTS, tm), F32)
    for hot in hots:
        sel = sel + jnp.where(hot, 1.0, 0.0)
    row = lax.broadcasted_iota(jnp.int32, (tm, tm), 0)
    col = lax.broadcasted_iota(jnp.int32, (tm, tm), 1)
    before = jnp.where(row < col, 1.0, 0.0).astype(BF16)
    pos = _dot(sel.astype(BF16), before) + carry_ref[:, 0:1]
    for k in range(TOP_K):
        rk = jnp.sum(jnp.where(hots[k], pos, 0.0), axis=0, keepdims=True)
        rank_ref[k:k + 1, :] = rk.astype(jnp.int32)
    carry_ref[...] = carry_ref[...] + jnp.sum(sel, axis=1, keepdims=True)
    cnt_ref[...] = carry_ref[...]


def _mix(ym, yg, proj, x2d, wbm, wbg, wout, ln_g, ln_b, wr_t, br, alpha):
    T, D = x2d.shape
    tm = min(MIX_TM, T)
    rowblk = lambda i: (i, 0)
    const = lambda i: (0, 0)
    once = dict(pipeline_mode=pl.Buffered(1))
    return pl.pallas_call(
        functools.partial(_mix_kernel, alpha),
        grid=(T // tm,),
        in_specs=[pl.BlockSpec((tm, M_VW), rowblk),
                  pl.BlockSpec((tm, G_VW), rowblk),
                  pl.BlockSpec((tm, D), lambda i: (i, 3)),
                  pl.BlockSpec((tm, D), lambda i: (i, 4)),
                  pl.BlockSpec((tm, D), rowblk),
                  pl.BlockSpec((M_VW, D), const, **once),
                  pl.BlockSpec((G_VW, D), const, **once),
                  pl.BlockSpec((D, D), const, **once),
                  pl.BlockSpec((1, D), const),
                  pl.BlockSpec((1, D), const),
                  pl.BlockSpec((N_EXPERTS, D), const),
                  pl.BlockSpec((N_EXPERTS, 1), const)],
        out_specs=[pl.BlockSpec((tm, D), rowblk),
                   pl.BlockSpec((TOP_K, tm), lambda i: (0, i)),
                   pl.BlockSpec((TOP_K, tm), lambda i: (0, i)),
                   pl.BlockSpec((TOP_K, tm), lambda i: (0, i)),
                   pl.BlockSpec((N_EXPERTS, LANES), const)],
        out_shape=[jax.ShapeDtypeStruct((T, D), F32),
                   jax.ShapeDtypeStruct((TOP_K, T), jnp.int32),
                   jax.ShapeDtypeStruct((TOP_K, T), F32),
                   jax.ShapeDtypeStruct((TOP_K, T), jnp.int32),
                   jax.ShapeDtypeStruct((N_EXPERTS, LANES), F32)],
        scratch_shapes=[pltpu.VMEM((N_EXPERTS, LANES), F32)],
        compiler_params=_params(("arbitrary",)),
    )(ym, yg, proj, proj, x2d, wbm, wbg, wout, ln_g, ln_b, wr_t, br)


def _row_copy(src, src_row, dst, dst_row, sem):
    return pltpu.make_async_copy(src.at[pl.ds(src_row, 1)], dst.at[pl.ds(dst_row, 1)], sem)


def _dispatch_kernel(n_tok, dest_ref, x1_ref, xs_hbm, sem):
    tm = x1_ref.shape[0]
    base = pl.program_id(0) * tm

    def issue(t, carry):
        for k in range(TOP_K):
            _row_copy(x1_ref, t, xs_hbm, dest_ref[k * n_tok + base + t], sem).start()
        return carry

    lax.fori_loop(0, tm, issue, 0, unroll=ROW_DMA_UNROLL)
    for k in range(TOP_K):
        pltpu.make_async_copy(x1_ref, xs_hbm.at[pl.ds(0, tm)], sem).wait()


def _dispatch(dest_flat, x1, n_rows):
    T, D = x1.shape
    tm = min(DISP_TM, T)
    return pl.pallas_call(
        functools.partial(_dispatch_kernel, T),
        grid_spec=pltpu.PrefetchScalarGridSpec(
            num_scalar_prefetch=1,
            grid=(T // tm,),
            in_specs=[pl.BlockSpec((tm, D), lambda i, d: (i, 0))],
            out_specs=pl.BlockSpec(memory_space=pl.ANY),
            scratch_shapes=[pltpu.SemaphoreType.DMA(())]),
        out_shape=jax.ShapeDtypeStruct((n_rows, D), x1.dtype),
        compiler_params=pltpu.CompilerParams(dimension_semantics=("arbitrary",), has_side_effects=True),
    )(dest_flat, x1)


def _zero_rows_kernel(start_ref, cnt_ref, xs_in_hbm, xs_hbm, zrow_ref, sem):
    del xs_in_hbm
    zrow_ref[...] = jnp.zeros_like(zrow_ref)

    def per_range(e, carry):
        start, cnt = start_ref[e], cnt_ref[e]
        head = jnp.minimum(cnt, (-start) & (SUBLANES - 1))
        mid = start + head
        n_blk = (cnt - head) // ZERO_CHUNK
        tail = mid + n_blk * ZERO_CHUNK
        n_tail = cnt - head - n_blk * ZERO_CHUNK

        def blk_copy(r):
            dst = pl.multiple_of(mid + r * ZERO_CHUNK, SUBLANES)
            return pltpu.make_async_copy(zrow_ref, xs_hbm.at[pl.ds(dst, ZERO_CHUNK)], sem)

        def rows(first, n):
            def issue(r, c):
                _row_copy(zrow_ref, 0, xs_hbm, first + r, sem).start()
                return c

            def drain(r, c):
                _row_copy(zrow_ref, 0, xs_hbm, 0, sem).wait()
                return c

            lax.fori_loop(0, n, issue, 0)
            lax.fori_loop(0, n, drain, 0)

        def issue_blk(r, c):
            blk_copy(r).start()
            return c

        def drain_blk(r, c):
            blk_copy(0).wait()
            return c

        rows(start, head)
        lax.fori_loop(0, n_blk, issue_blk, 0)
        lax.fori_loop(0, n_blk, drain_blk, 0)
        rows(tail, n_tail)
        return carry

    lax.fori_loop(0, start_ref.shape[0], per_range, 0)


def _zero_rows(start, cnt, xs):
    return pl.pallas_call(
        _zero_rows_kernel,
        grid_spec=pltpu.PrefetchScalarGridSpec(
            num_scalar_prefetch=2,
            grid=(1,),
            in_specs=[pl.BlockSpec(memory_space=pl.ANY)],
            out_specs=pl.BlockSpec(memory_space=pl.ANY),
            scratch_shapes=[pltpu.VMEM((ZERO_CHUNK, xs.shape[1]), xs.dtype), pltpu.SemaphoreType.DMA(())]),
        out_shape=jax.ShapeDtypeStruct(xs.shape, xs.dtype),
        input_output_aliases={2: 0},
        compiler_params=pltpu.CompilerParams(dimension_semantics=("arbitrary",), has_side_effects=True),
    )(start, cnt, xs)


def _expert_kernel(te_ref, na_ref, xs_ref, wgu_hbm, bgu_ref, wd_hbm, bd_ref, o_ref, wgu_ref, wd_ref, stage_ref, sems):
    i = pl.program_id(0)
    fh, d = wd_ref.shape
    active = i < na_ref[0]
    e = te_ref[i]
    changed = jnp.logical_or(i == 0, e != te_ref[jnp.maximum(i - 1, 0)])

    def swiglu(a, tf, f):
        glu = slice(f * tf, (f + 1) * tf)
        lin = slice(fh + f * tf, fh + (f + 1) * tf)
        g = _dot(a, wgu_ref[:, glu]) + bgu_ref[:, glu]
        l = _dot(a, wgu_ref[:, lin]) + bgu_ref[:, lin]
        a_glu = jnp.minimum(g, SWIGLU_LIMIT)
        a_lin = jnp.clip(l, -SWIGLU_LIMIT, SWIGLU_LIMIT)
        return (a_glu * _sigmoid(SWIGLU_ALPHA * a_glu) * (a_lin + 1.0)).astype(BF16)

    @pl.when(jnp.logical_and(active, changed))
    def _():
        ts = stage_ref.shape[2]
        nf = fh // ts
        n_piece = 3 * nf

        def copies(p):
            b = p % 2
            if p < 2 * nf:
                f, lin = divmod(p, 2)
                c0 = lin * fh + f * ts
                return [pltpu.make_async_copy(wgu_hbm.at[e, :, pl.ds(c0, ts)], stage_ref.at[b], sems.at[b])]
            r0 = (p - 2 * nf) * ts
            return [pltpu.make_async_copy(wd_hbm.at[e, pl.ds(r0, ts), pl.ds(j * ts, ts)],
                                          stage_ref.at[b, pl.ds(j * ts, ts), :], sems.at[b])
                    for j in range(d // ts)]

        def land(p):
            for cp in copies(p):
                cp.wait()
            b = p % 2
            if p < 2 * nf:
                f, lin = divmod(p, 2)
                c0 = lin * fh + f * ts
                wgu_ref[:, c0:c0 + ts] = stage_ref[b].astype(BF16)
            else:
                r0 = (p - 2 * nf) * ts
                for j in range(d // ts):
                    wd_ref[r0:r0 + ts, j * ts:(j + 1) * ts] = stage_ref[b, j * ts:(j + 1) * ts, :].astype(BF16)

        for p in range(2):
            for cp in copies(p):
                cp.start()
        a = xs_ref[...].astype(BF16)
        acts, acc = [], None
        for p in range(n_piece):
            land(p)
            if p + 2 < n_piece:
                for cp in copies(p + 2):
                    cp.start()
            if p < 2 * nf and p % 2 == 1:
                acts.append(swiglu(a, ts, p // 2))
            if p >= 2 * nf:
                f = p - 2 * nf
                part = _dot(acts[f], wd_ref[f * ts:(f + 1) * ts, :])
                acc = part if acc is None else acc + part
        o_ref[...] = acc + bd_ref[...]

    @pl.when(jnp.logical_and(active, jnp.logical_not(changed)))
    def _():
        tf = min(EXP_TF, fh)
        a = xs_ref[...].astype(BF16)
        acc = None
        for f in range(fh // tf):
            part = _dot(swiglu(a, tf, f), wd_ref[f * tf:(f + 1) * tf, :])
            acc = part if acc is None else acc + part
        o_ref[...] = acc + bd_ref[...]

    @pl.when(i >= na_ref[0])
    def _():
        o_ref[...] = jnp.zeros_like(o_ref)


def _experts(tile_e, n_active, xs, w_gu, b_gu, w_down, b_down):
    P, D = xs.shape
    E, _, F2 = w_gu.shape
    Fh = F2 // 2
    n_tiles = P // EXP_TM

    def tile(i, na):
        return jnp.minimum(i, na[0] - 1)

    return pl.pallas_call(
        _expert_kernel,
        grid_spec=pltpu.PrefetchScalarGridSpec(
            num_scalar_prefetch=2,
            grid=(n_tiles,),
            in_specs=[
                pl.BlockSpec((EXP_TM, D), lambda i, te, na: (tile(i, na), 0)),
                pl.BlockSpec(memory_space=pl.ANY),
                pl.BlockSpec((None, 1, F2), lambda i, te, na: (te[tile(i, na)], 0, 0)),
                pl.BlockSpec(memory_space=pl.ANY),
                pl.BlockSpec((None, 1, D), lambda i, te, na: (te[tile(i, na)], 0, 0)),
            ],
            out_specs=pl.BlockSpec((EXP_TM, D), lambda i, te, na: (i, 0)),
            scratch_shapes=[pltpu.VMEM((D, F2), BF16),
                            pltpu.VMEM((Fh, D), BF16),
                            pltpu.VMEM((2, D, EXP_WPIECE), F32),
                            pltpu.SemaphoreType.DMA((2,))]),
        out_shape=jax.ShapeDtypeStruct((P, D), F32),
        compiler_params=_params(("arbitrary",), EXP_VMEM_LIMIT),
    )(tile_e, n_active, xs, w_gu, b_gu, w_down, b_down)


def _combine_kernel(alpha, n_tok, dest_ref, ys_hbm, x1_ref, gate_ref, lg_ref, lb_ref, o_ref, rows_ref, sems):
    tm = x1_ref.shape[0]
    i = pl.program_id(0)
    buf = i % 2

    def gather(tile, b):
        base = tile * tm

        def issue(t, carry):
            for k in range(TOP_K):
                _row_copy(ys_hbm, dest_ref[k * n_tok + base + t], rows_ref.at[b, k], t, sems.at[b]).start()
            return carry

        lax.fori_loop(0, tm, issue, 0, unroll=ROW_DMA_UNROLL)

    @pl.when(i == 0)
    def _():
        gather(0, 0)

    @pl.when(i + 1 < pl.num_programs(0))
    def _():
        gather(i + 1, 1 - buf)

    for k in range(TOP_K):
        pltpu.make_async_copy(ys_hbm.at[pl.ds(0, tm)], rows_ref.at[buf, k], sems.at[buf]).wait()

    ff = gate_ref[:, 0:1] * rows_ref[buf, 0]
    for k in range(1, TOP_K):
        ff = ff + gate_ref[:, k:k + 1] * rows_ref[buf, k]
    o_ref[...] = _layer_norm(alpha * x1_ref[...] + ff, lg_ref[...], lb_ref[...])


def _combine(dest_flat, ys, x1, gates_col, ln_g, ln_b, alpha):
    T, D = x1.shape
    tm = min(COMB_TM, T)
    return pl.pallas_call(
        functools.partial(_combine_kernel, alpha, T),
        grid_spec=pltpu.PrefetchScalarGridSpec(
            num_scalar_prefetch=1,
            grid=(T // tm,),
            in_specs=[pl.BlockSpec(memory_space=pl.ANY),
                      pl.BlockSpec((tm, D), lambda i, d: (i, 0)),
                      pl.BlockSpec((tm, TOP_K), lambda i, d: (i, 0)),
                      pl.BlockSpec((1, D), lambda i, d: (0, 0)),
                      pl.BlockSpec((1, D), lambda i, d: (0, 0))],
            out_specs=pl.BlockSpec((tm, D), lambda i, d: (i, 0)),
            scratch_shapes=[pltpu.VMEM((2, TOP_K, tm, D), F32), pltpu.SemaphoreType.DMA((2,))]),
        out_shape=jax.ShapeDtypeStruct((T, D), F32),
        compiler_params=_params(("arbitrary",)),
    )(dest_flat, ys, x1, gates_col, ln_g, ln_b)


def _layer(x2d, B, S, alpha, w_in, b_in, conv_w, conv_b, w_gla_gate, b_gla_gate, m_norm_w, g_norm_w,
           w_branch_m, w_branch_g, w_out, ln1_g, ln1_b, w_router, b_router, w_gu, b_gu, w_down, b_down,
           ln2_g, ln2_b):
    T, D = x2d.shape
    m_end = 2 * M_QKW + 2 * M_VW
    g_beg = m_end + 2 * M_HEADS
    g_end = g_beg + 2 * G_KW + 2 * G_VW
    mg_beg = g_end + G_RANK
    main_cols = [(0, m_end), (g_beg, g_end), (mg_beg, mg_beg + 2 * D)]
    w_main = jnp.concatenate([w_in[:, a:b] for a, b in main_cols], axis=1).astype(BF16)
    b_main = jnp.concatenate([b_in[a:b] for a, b in main_cols])[None, :]
    n_small = 2 * M_HEADS + G_RANK
    w_small = jnp.concatenate([w_in[:, m_end:g_beg], w_in[:, g_end:mg_beg],
                               jnp.zeros((D, SMALL_W - n_small), F32)], axis=1).astype(BF16)
    b_small = jnp.concatenate([b_in[m_end:g_beg], b_in[g_end:mg_beg], jnp.zeros((SMALL_W - n_small,), F32)])[None, :]

    proj, small = _proj(x2d, w_main, b_main, w_small, b_small)
    small_t = small[:, :SUBLANES].T

    ym = _mlstm(proj, small, small_t, conv_w, conv_b[None, :], m_norm_w[None, :], B, S)
    wg_pad = jnp.zeros((SMALL_W, G_KW), F32).at[2 * M_HEADS:n_small].set(w_gla_gate).astype(BF16)
    yg = _gla(proj, small, wg_pad, b_gla_gate[None, :], g_norm_w[None, :], B, S)

    x1, top_e, gates, rank, cnt = _mix(
        ym, yg, proj, x2d, w_branch_m.astype(BF16), w_branch_g.astype(BF16), w_out.astype(BF16),
        ln1_g[None, :], ln1_b[None, :], w_router.T.astype(BF16), b_router[:, None], alpha)

    counts = cnt[:, 0].astype(jnp.int32)
    padded = (counts + EXP_TM - 1) // EXP_TM * EXP_TM
    pend = jnp.cumsum(padded)
    pstart = pend - padded
    n_tiles = (T * TOP_K) // EXP_TM + N_EXPERTS
    tile_start = jnp.arange(n_tiles, dtype=jnp.int32) * EXP_TM
    tile_e = jnp.sum((pend[None, :] <= tile_start[:, None]).astype(jnp.int32), axis=1)
    tile_e = jnp.minimum(tile_e, N_EXPERTS - 1)
    n_active = (pend[-1:] // EXP_TM).astype(jnp.int32)
    dest = rank
    for e in range(N_EXPERTS):
        dest = dest + jnp.where(top_e == e, pstart[e], 0)
    dest = dest.reshape(-1)

    xs = _dispatch(dest, x1, n_tiles * EXP_TM)
    n_rows = n_tiles * EXP_TM
    hole_start = jnp.concatenate([pstart + counts, pend[-1:]])
    hole_cnt = jnp.concatenate([padded - counts, n_rows - pend[-1:]])
    xs = _zero_rows(hole_start, hole_cnt, xs)
    ys = _experts(tile_e, n_active, xs, w_gu, b_gu[:, None, :], w_down, b_down[:, None, :])
    return _combine(dest, ys, x1, gates.T, ln2_g[None, :], ln2_b[None, :], alpha)


def kernel(x, w_in, b_in, conv_w, conv_b, w_gla_gate, b_gla_gate, m_norm_w, g_norm_w, w_branch_m, w_branch_g, w_out, ln1_g, ln1_b, w_router, b_router, w_gu, b_gu, w_down, b_down, ln2_g, ln2_b):
    B, S, D = x.shape
    depth = w_in.shape[0]
    alpha = (2 * depth) ** 0.25
    x2d = x.reshape(B * S, D)
    for l in range(depth):
        x2d = _layer(x2d, B, S, alpha, w_in[l], b_in[l], conv_w[l], conv_b[l], w_gla_gate[l], b_gla_gate[l],
                     m_norm_w[l], g_norm_w[l], w_branch_m[l], w_branch_g[l], w_out[l], ln1_g[l], ln1_b[l],
                     w_router[l], b_router[l], w_gu[l], b_gu[l], w_down[l], b_down[l], ln2_g[l], ln2_b[l])
    return x2d.reshape(B, S, D)
```

```python
import functools

import jax
import jax.numpy as jnp
from jax import lax
from jax.experimental import pallas as pl
from jax.experimental.pallas import tpu as pltpu

F32 = jnp.float32
BF16 = jnp.bfloat16

M_HEADS, M_QK, M_V, CONV_W = 4, 128, 256, 4
G_HEADS, G_K, G_V, G_RANK, G_TAU = 4, 128, 256, 16, 16.0
N_EXPERTS, TOP_K = 32, 4
SWIGLU_ALPHA, SWIGLU_LIMIT = 1.702, 7.0
NORM_EPS = 1e-5
M_QKW, M_VW = M_HEADS * M_QK, M_HEADS * M_V
G_KW, G_VW = G_HEADS * G_K, G_HEADS * G_V

LANES = 128
SUBLANES = 8
VMEM_LIMIT = 56 * 1024 * 1024
EXP_VMEM_LIMIT = 60 * 1024 * 1024

PROJ_TM, PROJ_TN = 1024, 1024
SMALL_W = LANES
M_CHUNK = 256
G_STEP = 256
G_CHUNK = 64
MIX_TM = 256
EXP_TM = 512
EXP_TF = 1024
EXP_WPIECE = 512
DISP_TM = 512
COMB_TM = 256
ROW_DMA_UNROLL = 8
ZERO_CHUNK = 64


def _dot(a, b):
    return jnp.dot(a, b, preferred_element_type=F32)


def _dot_nt(a, b):
    return lax.dot_general(a, b, (((1,), (1,)), ((), ())), preferred_element_type=F32)


def _dot_tn(a, b):
    return lax.dot_general(a, b, (((0,), (0,)), ((), ())), preferred_element_type=F32)


def _split3(x):
    hi = x.astype(BF16)
    r = x - hi.astype(F32)
    mid = r.astype(BF16)
    lo = (r - mid.astype(F32)).astype(BF16)
    return hi, mid, lo


def _sigmoid(x):
    return 0.5 * jnp.tanh(0.5 * x) + 0.5


def _log_sigmoid(x):
    return jnp.minimum(x, 0.0) - jnp.log1p(jnp.exp(-jnp.abs(x)))


def _params(sem, vmem_limit=VMEM_LIMIT):
    return pltpu.CompilerParams(dimension_semantics=sem, vmem_limit_bytes=vmem_limit)


def _proj_kernel(x_ref, w_ref, b_ref, ws_ref, bs_ref, o_ref, os_ref, xb_ref):
    @pl.when(pl.program_id(1) == 0)
    def _():
        xb_ref[...] = x_ref[...].astype(BF16)
        os_ref[...] = _dot(xb_ref[...], ws_ref[...]) + bs_ref[...]

    o_ref[...] = _dot(xb_ref[...], w_ref[...]) + b_ref[...]


def _proj(x, w_bf, b, ws_bf, bs):
    T, K = x.shape
    N = w_bf.shape[1]
    tm = min(PROJ_TM, T)
    tn = PROJ_TN
    return pl.pallas_call(
        _proj_kernel,
        grid=(T // tm, N // tn),
        in_specs=[pl.BlockSpec((tm, K), lambda i, j: (i, 0)),
                  pl.BlockSpec((K, tn), lambda i, j: (0, j)),
                  pl.BlockSpec((1, tn), lambda i, j: (0, j)),
                  pl.BlockSpec((K, SMALL_W), lambda i, j: (0, 0)),
                  pl.BlockSpec((1, SMALL_W), lambda i, j: (0, 0))],
        out_specs=[pl.BlockSpec((tm, tn), lambda i, j: (i, j)),
                   pl.BlockSpec((tm, SMALL_W), lambda i, j: (i, 0))],
        out_shape=[jax.ShapeDtypeStruct((T, N), F32), jax.ShapeDtypeStruct((T, SMALL_W), F32)],
        scratch_shapes=[pltpu.VMEM((tm, K), BF16)],
        compiler_params=_params(("parallel", "arbitrary")),
    )(x, w_bf, b, ws_bf, bs)


def _mlstm_kernel(qk_ref, v_ref, og_ref, sm_ref, smt_ref, cw_ref, cb_ref, nw_ref, o_ref,
                  ext_ref, c_ref, n_ref, m_ref):
    L = M_CHUNK
    c = pl.program_id(1)

    @pl.when(c == 0)
    def _():
        ext_ref[0:SUBLANES, :] = jnp.zeros((SUBLANES, 2 * M_QKW), F32)
        c_ref[...] = jnp.zeros_like(c_ref)
        n_ref[...] = jnp.zeros_like(n_ref)
        m_ref[...] = jnp.zeros_like(m_ref)

    ext_ref[SUBLANES:, :] = qk_ref[...]
    acc = cb_ref[...] + cw_ref[CONV_W - 1:CONV_W, :] * ext_ref[SUBLANES:, :]
    for j in range(CONV_W - 1):
        off = SUBLANES - (CONV_W - 1) + j
        acc = acc + cw_ref[j:j + 1, :] * ext_ref[off:off + L, :]
    ext_ref[0:SUBLANES, :] = ext_ref[L:L + SUBLANES, :]
    qk = acc * _sigmoid(acc)

    row = lax.broadcasted_iota(jnp.int32, (L, L), 0)
    col = lax.broadcasted_iota(jnp.int32, (L, L), 1)
    causal = row >= col
    tri = jnp.where(causal, 1.0, 0.0).astype(BF16)
    triu = jnp.where(col >= row, 1.0, 0.0).astype(BF16)

    sm = sm_ref[...]
    smt = smt_ref[...]
    h1, h2, h3 = _split3(_log_sigmoid(sm))
    bcol_all = _dot(tri, h1) + _dot(tri, h2) + _dot(tri, h3)
    r1, r2, r3 = _split3(_log_sigmoid(smt))
    brow_all = _dot(r1, triu) + _dot(r2, triu) + _dot(r3, triu)

    for h in range(M_HEADS):
        q = qk[:, h * M_QK:(h + 1) * M_QK]
        k = qk[:, M_QKW + h * M_QK:M_QKW + (h + 1) * M_QK] * (M_QK ** -0.5)
        v = v_ref[:, h * M_V:(h + 1) * M_V]
        qb, kb, vb = q.astype(BF16), k.astype(BF16), v.astype(BF16)
        li_col = sm[:, h:h + 1]
        b_col = bcol_all[:, M_HEADS + h:M_HEADS + h + 1]
        b_end = b_col[L - 1:L, :]
        li_row = smt[h:h + 1, :]
        b_row = brow_all[M_HEADS + h:M_HEADS + h + 1, :]
        m_prev = m_ref[h:h + 1, 0:1]
        n_prev = n_ref[h:h + 1, :]
        c_prev = c_ref[h]

        d_log = jnp.where(causal, b_col + (li_row - b_row), -jnp.inf)
        m_inter = b_col + m_prev
        m_t = jnp.maximum(m_inter, jnp.max(d_log, axis=-1, keepdims=True))
        dec = jnp.exp(m_inter - m_t)
        s = _dot_nt(qb, kb) * jnp.exp(d_log - m_t)
        num = _dot(s.astype(BF16), vb) + dec * _dot(qb, c_prev.astype(BF16))
        den = jnp.sum(s, axis=-1, keepdims=True) + dec * jnp.sum(q * n_prev, axis=-1, keepdims=True)
        hh = num / jnp.maximum(jnp.abs(den), jnp.exp(-m_t))

        w_end = b_end - b_col + li_col
        g_end = jnp.max(w_end, axis=0, keepdims=True)
        ke = k * jnp.exp(w_end - g_end)
        d_c = _dot_tn(ke.astype(BF16), vb)
        d_n = jnp.sum(ke, axis=0, keepdims=True)
        m_new = jnp.maximum(b_end + m_prev, g_end)
        a = jnp.exp(b_end + m_prev - m_new)
        cc = jnp.exp(g_end - m_new)
        c_ref[h] = a * c_prev + cc * d_c
        n_ref[h:h + 1, :] = a * n_prev + cc * d_n
        m_ref[h:h + 1, :] = jnp.broadcast_to(m_new, (1, LANES))

        mu = jnp.mean(hh, axis=-1, keepdims=True)
        xc = hh - mu
        var = jnp.mean(xc * xc, axis=-1, keepdims=True)
        sl = slice(h * M_V, (h + 1) * M_V)
        y = xc * lax.rsqrt(var + NORM_EPS) * nw_ref[:, sl] * _sigmoid(og_ref[:, sl])
        o_ref[:, sl] = y.astype(o_ref.dtype)


def _mlstm(proj, small, small_t, conv_w, conv_b, norm_w, B, S):
    L = M_CHUNK
    NC = S // L
    T = B * S
    rowblk = lambda b, c: (b * NC + c, 0)
    const = lambda b, c: (0, 0)
    return pl.pallas_call(
        _mlstm_kernel,
        grid=(B, NC),
        in_specs=[pl.BlockSpec((L, 2 * M_QKW), rowblk),
                  pl.BlockSpec((L, M_VW), lambda b, c: (b * NC + c, 1)),
                  pl.BlockSpec((L, M_VW), lambda b, c: (b * NC + c, 2)),
                  pl.BlockSpec((L, SMALL_W), rowblk),
                  pl.BlockSpec((SUBLANES, L), lambda b, c: (0, b * NC + c)),
                  pl.BlockSpec((CONV_W, 2 * M_QKW), const),
                  pl.BlockSpec((1, 2 * M_QKW), const),
                  pl.BlockSpec((1, M_VW), const)],
        out_specs=pl.BlockSpec((L, M_VW), rowblk),
        out_shape=jax.ShapeDtypeStruct((T, M_VW), BF16),
        scratch_shapes=[pltpu.VMEM((L + SUBLANES, 2 * M_QKW), F32),
                        pltpu.VMEM((M_HEADS, M_QK, M_V), F32),
                        pltpu.VMEM((SUBLANES, M_QK), F32),
                        pltpu.VMEM((SUBLANES, LANES), F32)],
        compiler_params=_params(("parallel", "arbitrary")),
    )(proj, proj, proj, small, small_t, conv_w, conv_b, norm_w)


def _gla_kernel(qk_ref, v_ref, gg_ref, sm_ref, wg_ref, bg_ref, nw_ref, o_ref, st_ref):
    LC = G_CHUNK
    c = pl.program_id(1)

    @pl.when(c == 0)
    def _():
        st_ref[...] = jnp.zeros_like(st_ref)

    pre = _dot(sm_ref[...].astype(BF16), wg_ref[...]) + bg_ref[...]
    log_a = _log_sigmoid(pre) / G_TAU

    row = lax.broadcasted_iota(jnp.int32, (LC, LC), 0)
    col = lax.broadcasted_iota(jnp.int32, (LC, LC), 1)
    causal = row >= col
    tri = jnp.where(causal, 1.0, 0.0).astype(BF16)

    for j in range(G_STEP // LC):
        rs = slice(j * LC, (j + 1) * LC)
        a1, a2, a3 = _split3(log_a[rs, :])
        cum = _dot(tri, a1) + _dot(tri, a2) + _dot(tri, a3)
        cum_end = cum[LC - 1:LC, :]
        e_q = jnp.exp(cum)
        e_k = jnp.exp(-cum)
        e_end = jnp.exp(cum_end - cum)
        e_dec = jnp.exp(cum_end)
        for h in range(G_HEADS):
            ks = slice(h * G_K, (h + 1) * G_K)
            vs = slice(h * G_V, (h + 1) * G_V)
            q = qk_ref[rs, ks] * (G_K ** -0.5)
            k = qk_ref[rs, G_KW + h * G_K:G_KW + (h + 1) * G_K]
            vb = v_ref[rs, vs].astype(BF16)
            q_in = (q * e_q[:, ks]).astype(BF16)
            k_in = (k * e_k[:, ks]).astype(BF16)
            k_end = (k * e_end[:, ks]).astype(BF16)
            st = st_ref[h]
            s = jnp.where(causal, _dot_nt(q_in, k_in), 0.0)
            o = _dot(s.astype(BF16), vb) + _dot_nt(q_in, st.astype(BF16))
            st_ref[h] = st * e_dec[:, ks] + _dot_tn(vb, k_end)
            ms = jnp.mean(o * o, axis=-1, keepdims=True)
            g = gg_ref[rs, vs]
            y = o * lax.rsqrt(ms + NORM_EPS) * nw_ref[:, vs] * (g * _sigmoid(g))
            o_ref[rs, vs] = y.astype(o_ref.dtype)


def _gla(proj, small, wg_pad, bg, norm_w, B, S):
    L = G_STEP
    NC = S // L
    T = B * S
    rowblk = lambda b, c: (b * NC + c, 0)
    const = lambda b, c: (0, 0)
    return pl.pallas_call(
        _gla_kernel,
        grid=(B, NC),
        in_specs=[pl.BlockSpec((L, 2 * G_KW), lambda b, c: (b * NC + c, 3)),
                  pl.BlockSpec((L, G_VW), lambda b, c: (b * NC + c, 4)),
                  pl.BlockSpec((L, G_VW), lambda b, c: (b * NC + c, 5)),
                  pl.BlockSpec((L, SMALL_W), rowblk),
                  pl.BlockSpec((SMALL_W, G_KW), const),
                  pl.BlockSpec((1, G_KW), const),
                  pl.BlockSpec((1, G_VW), const)],
        out_specs=pl.BlockSpec((L, G_VW), rowblk),
        out_shape=jax.ShapeDtypeStruct((T, G_VW), BF16),
        scratch_shapes=[pltpu.VMEM((G_HEADS, G_V, G_K), F32)],
        compiler_params=_params(("parallel", "arbitrary")),
    )(proj, proj, proj, small, wg_pad, bg, norm_w)


def _layer_norm(r, g, b):
    mu = jnp.mean(r, axis=-1, keepdims=True)
    xc = r - mu
    var = jnp.mean(xc * xc, axis=-1, keepdims=True)
    return xc * lax.rsqrt(var + NORM_EPS) * g + b


def _mix_kernel(alpha, ym_ref, yg_ref, gm_ref, gg_ref, x_ref, wbm_ref, wbg_ref, wout_ref, lg_ref, lb_ref,
                wr_ref, br_ref, x1_ref, tope_ref, gate_ref, rank_ref, cnt_ref, carry_ref):
    tm = x_ref.shape[0]
    i = pl.program_id(0)

    @pl.when(i == 0)
    def _():
        carry_ref[...] = jnp.zeros_like(carry_ref)

    pm = _dot(ym_ref[...], wbm_ref[...])
    pg = _dot(yg_ref[...], wbg_ref[...])
    z = _sigmoid(gm_ref[...]) * pm + _sigmoid(gg_ref[...]) * pg
    mix = _dot(z.astype(BF16), wout_ref[...])
    x1 = _layer_norm(alpha * x_ref[...] + mix, lg_ref[...], lb_ref[...])
    x1_ref[...] = x1

    logits = _dot_nt(wr_ref[...], x1.astype(BF16)) + br_ref[...]
    eidx = lax.broadcasted_iota(jnp.int32, (N_EXPERTS, tm), 0)
    vals, hots = [], []
    cur = logits
    for k in range(TOP_K):
        mx = jnp.max(cur, axis=0, keepdims=True)
        idx = jnp.min(jnp.where(cur == mx, eidx, N_EXPERTS), axis=0, keepdims=True)
        hot = eidx == idx
        cur = jnp.where(hot, -jnp.inf, cur)
        vals.append(mx)
        hots.append(hot)
        tope_ref[k:k + 1, :] = idx
    exps = [jnp.exp(v - vals[0]) for v in vals]
    tot = exps[0] + exps[1] + exps[2] + exps[3]
    for k in range(TOP_K):
        gate_ref[k:k + 1, :] = exps[k] / tot

    sel = jnp.zeros((N_EXPERTS, tm), F32)
    for hot in hots:
        sel = sel + jnp.where(hot, 1.0, 0.0)
    row = lax.broadcasted_iota(jnp.int32, (tm, tm), 0)
    col = lax.broadcasted_iota(jnp.int32, (tm, tm), 1)
    before = jnp.where(row < col, 1.0, 0.0).astype(BF16)
    pos = _dot(sel.astype(BF16), before) + carry_ref[:, 0:1]
    for k in range(TOP_K):
        rk = jnp.sum(jnp.where(hots[k], pos, 0.0), axis=0, keepdims=True)
        rank_ref[k:k + 1, :] = rk.astype(jnp.int32)
    carry_ref[...] = carry_ref[...] + jnp.sum(sel, axis=1, keepdims=True)
    cnt_ref[...] = carry_ref[...]


def _mix(ym, yg, proj, x2d, wbm, wbg, wout, ln_g, ln_b, wr_t, br, alpha):
    T, D = x2d.shape
    tm = min(MIX_TM, T)
    rowblk = lambda i: (i, 0)
    const = lambda i: (0, 0)
    once = dict(pipeline_mode=pl.Buffered(1))
    return pl.pallas_call(
        functools.partial(_mix_kernel, alpha),
        grid=(T // tm,),
        in_specs=[pl.BlockSpec((tm, M_VW), rowblk),
                  pl.BlockSpec((tm, G_VW), rowblk),
                  pl.BlockSpec((tm, D), lambda i: (i, 3)),
                  pl.BlockSpec((tm, D), lambda i: (i, 4)),
                  pl.BlockSpec((tm, D), rowblk),
                  pl.BlockSpec((M_VW, D), const, **once),
                  pl.BlockSpec((G_VW, D), const, **once),
                  pl.BlockSpec((D, D), const, **once),
                  pl.BlockSpec((1, D), const),
                  pl.BlockSpec((1, D), const),
                  pl.BlockSpec((N_EXPERTS, D), const),
                  pl.BlockSpec((N_EXPERTS, 1), const)],
        out_specs=[pl.BlockSpec((tm, D), rowblk),
                   pl.BlockSpec((TOP_K, tm), lambda i: (0, i)),
                   pl.BlockSpec((TOP_K, tm), lambda i: (0, i)),
                   pl.BlockSpec((TOP_K, tm), lambda i: (0, i)),
                   pl.BlockSpec((N_EXPERTS, LANES), const)],
        out_shape=[jax.ShapeDtypeStruct((T, D), F32),
                   jax.ShapeDtypeStruct((TOP_K, T), jnp.int32),
                   jax.ShapeDtypeStruct((TOP_K, T), F32),
                   jax.ShapeDtypeStruct((TOP_K, T), jnp.int32),
                   jax.ShapeDtypeStruct((N_EXPERTS, LANES), F32)],
        scratch_shapes=[pltpu.VMEM((N_EXPERTS, LANES), F32)],
        compiler_params=_params(("arbitrary",)),
    )(ym, yg, proj, proj, x2d, wbm, wbg, wout, ln_g, ln_b, wr_t, br)


def _row_copy(src, src_row, dst, dst_row, sem):
    return pltpu.make_async_copy(src.at[pl.ds(src_row, 1)], dst.at[pl.ds(dst_row, 1)], sem)


def _dispatch_kernel(n_tok, dest_ref, x1_ref, xs_hbm, sem):
    tm = x1_ref.shape[0]
    base = pl.program_id(0) * tm

    def issue(t, carry):
        for k in range(TOP_K):
            _row_copy(x1_ref, t, xs_hbm, dest_ref[k * n_tok + base + t], sem).start()
        return carry

    lax.fori_loop(0, tm, issue, 0, unroll=ROW_DMA_UNROLL)
    for k in range(TOP_K):
        pltpu.make_async_copy(x1_ref, xs_hbm.at[pl.ds(0, tm)], sem).wait()


def _dispatch(dest_flat, x1, n_rows):
    T, D = x1.shape
    tm = min(DISP_TM, T)
    return pl.pallas_call(
        functools.partial(_dispatch_kernel, T),
        grid_spec=pltpu.PrefetchScalarGridSpec(
            num_scalar_prefetch=1,
            grid=(T // tm,),
            in_specs=[pl.BlockSpec((tm, D), lambda i, d: (i, 0))],
            out_specs=pl.BlockSpec(memory_space=pl.ANY),
            scratch_shapes=[pltpu.SemaphoreType.DMA(())]),
        out_shape=jax.ShapeDtypeStruct((n_rows, D), x1.dtype),
        compiler_params=pltpu.CompilerParams(dimension_semantics=("arbitrary",), has_side_effects=True),
    )(dest_flat, x1)


def _zero_rows_kernel(start_ref, cnt_ref, xs_in_hbm, xs_hbm, zrow_ref, sem):
    del xs_in_hbm
    zrow_ref[...] = jnp.zeros_like(zrow_ref)

    def per_range(e, carry):
        start, cnt = start_ref[e], cnt_ref[e]
        head = jnp.minimum(cnt, (-start) & (SUBLANES - 1))
        mid = start + head
        n_blk = (cnt - head) // ZERO_CHUNK
        tail = mid + n_blk * ZERO_CHUNK
        n_tail = cnt - head - n_blk * ZERO_CHUNK

        def blk_copy(r):
            dst = pl.multiple_of(mid + r * ZERO_CHUNK, SUBLANES)
            return pltpu.make_async_copy(zrow_ref, xs_hbm.at[pl.ds(dst, ZERO_CHUNK)], sem)

        def rows(first, n):
            def issue(r, c):
                _row_copy(zrow_ref, 0, xs_hbm, first + r, sem).start()
                return c

            def drain(r, c):
                _row_copy(zrow_ref, 0, xs_hbm, 0, sem).wait()
                return c

            lax.fori_loop(0, n, issue, 0)
            lax.fori_loop(0, n, drain, 0)

        def issue_blk(r, c):
            blk_copy(r).start()
            return c

        def drain_blk(r, c):
            blk_copy(0).wait()
            return c

        rows(start, head)
        lax.fori_loop(0, n_blk, issue_blk, 0)
        lax.fori_loop(0, n_blk, drain_blk, 0)
        rows(tail, n_tail)
        return carry

    lax.fori_loop(0, start_ref.shape[0], per_range, 0)


def _zero_rows(start, cnt, xs):
    return pl.pallas_call(
        _zero_rows_kernel,
        grid_spec=pltpu.PrefetchScalarGridSpec(
            num_scalar_prefetch=2,
            grid=(1,),
            in_specs=[pl.BlockSpec(memory_space=pl.ANY)],
            out_specs=pl.BlockSpec(memory_space=pl.ANY),
            scratch_shapes=[pltpu.VMEM((ZERO_CHUNK, xs.shape[1]), xs.dtype), pltpu.SemaphoreType.DMA(())]),
        out_shape=jax.ShapeDtypeStruct(xs.shape, xs.dtype),
        input_output_aliases={2: 0},
        compiler_params=pltpu.CompilerParams(dimension_semantics=("arbitrary",), has_side_effects=True),
    )(start, cnt, xs)


def _expert_kernel(te_ref, tr_ref, na_ref, xs_ref, wgu_hbm, bgu_ref, wd_hbm, bd_ref, o_ref,
                   wgu_ref, wd_ref, stage_ref, sems):
    i = pl.program_id(0)
    fh, d = wd_ref.shape
    active = i < na_ref[0]
    e = te_ref[i]
    changed = jnp.logical_or(i == 0, e != te_ref[jnp.maximum(i - 1, 0)])

    def swiglu(a, tf, f):
        glu = slice(f * tf, (f + 1) * tf)
        lin = slice(fh + f * tf, fh + (f + 1) * tf)
        g = _dot(a, wgu_ref[:, glu]) + bgu_ref[:, glu]
        l = _dot(a, wgu_ref[:, lin]) + bgu_ref[:, lin]
        a_glu = jnp.minimum(g, SWIGLU_LIMIT)
        a_lin = jnp.clip(l, -SWIGLU_LIMIT, SWIGLU_LIMIT)
        return (a_glu * _sigmoid(SWIGLU_ALPHA * a_glu) * (a_lin + 1.0)).astype(BF16)

    @pl.when(jnp.logical_and(active, changed))
    def _():
        ts = stage_ref.shape[2]
        nf = fh // ts
        n_piece = 3 * nf

        def copies(p):
            b = p % 2
            if p < 2 * nf:
                f, lin = divmod(p, 2)
                c0 = lin * fh + f * ts
                return [pltpu.make_async_copy(wgu_hbm.at[e, :, pl.ds(c0, ts)], stage_ref.at[b], sems.at[b])]
            r0 = (p - 2 * nf) * ts
            return [pltpu.make_async_copy(wd_hbm.at[e, pl.ds(r0, ts), pl.ds(j * ts, ts)],
                                          stage_ref.at[b, pl.ds(j * ts, ts), :], sems.at[b])
                    for j in range(d // ts)]

        def land(p):
            for cp in copies(p):
                cp.wait()
            b = p % 2
            if p < 2 * nf:
                f, lin = divmod(p, 2)
                c0 = lin * fh + f * ts
                wgu_ref[:, c0:c0 + ts] = stage_ref[b].astype(BF16)
            else:
                r0 = (p - 2 * nf) * ts
                for j in range(d // ts):
                    wd_ref[r0:r0 + ts, j * ts:(j + 1) * ts] = stage_ref[b, j * ts:(j + 1) * ts, :].astype(BF16)

        for p in range(2):
            for cp in copies(p):
                cp.start()
        a = xs_ref[...].astype(BF16)
        acts, acc = [], None
        for p in range(n_piece):
            land(p)
            if p + 2 < n_piece:
                for cp in copies(p + 2):
                    cp.start()
            if p < 2 * nf and p % 2 == 1:
                acts.append(swiglu(a, ts, p // 2))
            if p >= 2 * nf:
                f = p - 2 * nf
                part = _dot(acts[f], wd_ref[f * ts:(f + 1) * ts, :])
                acc = part if acc is None else acc + part
        o_ref[...] = acc + bd_ref[...]

    def ffn(a):
        tf = min(EXP_TF, fh)
        acc = None
        for f in range(fh // tf):
            part = _dot(swiglu(a, tf, f), wd_ref[f * tf:(f + 1) * tf, :])
            acc = part if acc is None else acc + part
        return acc + bd_ref[...]

    half = xs_ref.shape[0] // 2
    later = jnp.logical_and(active, jnp.logical_not(changed))
    short = tr_ref[i] <= half

    @pl.when(jnp.logical_and(later, jnp.logical_not(short)))
    def _():
        o_ref[...] = ffn(xs_ref[...].astype(BF16))

    @pl.when(jnp.logical_and(later, short))
    def _():
        o_ref[0:half, :] = ffn(xs_ref[0:half, :].astype(BF16))
        o_ref[half:, :] = jnp.zeros((half, d), F32)

    @pl.when(i >= na_ref[0])
    def _():
        o_ref[...] = jnp.zeros_like(o_ref)


def _experts(tile_e, tile_rows, n_active, xs, w_gu, b_gu, w_down, b_down):
    P, D = xs.shape
    E, _, F2 = w_gu.shape
    Fh = F2 // 2
    n_tiles = P // EXP_TM

    def tile(i, na):
        return jnp.minimum(i, na[0] - 1)

    return pl.pallas_call(
        _expert_kernel,
        grid_spec=pltpu.PrefetchScalarGridSpec(
            num_scalar_prefetch=3,
            grid=(n_tiles,),
            in_specs=[
                pl.BlockSpec((EXP_TM, D), lambda i, te, tr, na: (tile(i, na), 0)),
                pl.BlockSpec(memory_space=pl.ANY),
                pl.BlockSpec((None, 1, F2), lambda i, te, tr, na: (te[tile(i, na)], 0, 0)),
                pl.BlockSpec(memory_space=pl.ANY),
                pl.BlockSpec((None, 1, D), lambda i, te, tr, na: (te[tile(i, na)], 0, 0)),
            ],
            out_specs=pl.BlockSpec((EXP_TM, D), lambda i, te, tr, na: (i, 0)),
            scratch_shapes=[pltpu.VMEM((D, F2), BF16),
                            pltpu.VMEM((Fh, D), BF16),
                            pltpu.VMEM((2, D, EXP_WPIECE), F32),
                            pltpu.SemaphoreType.DMA((2,))]),
        out_shape=jax.ShapeDtypeStruct((P, D), F32),
        compiler_params=_params(("arbitrary",), EXP_VMEM_LIMIT),
    )(tile_e, tile_rows, n_active, xs, w_gu, b_gu, w_down, b_down)


def _combine_kernel(alpha, n_tok, dest_ref, ys_hbm, x1_ref, gate_ref, lg_ref, lb_ref, o_ref, rows_ref, sems):
    tm = x1_ref.shape[0]
    i = pl.program_id(0)
    buf = i % 2

    def gather(tile, b):
        base = tile * tm

        def issue(t, carry):
            for k in range(TOP_K):
                _row_copy(ys_hbm, dest_ref[k * n_tok + base + t], rows_ref.at[b, k], t, sems.at[b]).start()
            return carry

        lax.fori_loop(0, tm, issue, 0, unroll=ROW_DMA_UNROLL)

    @pl.when(i == 0)
    def _():
        gather(0, 0)

    @pl.when(i + 1 < pl.num_programs(0))
    def _():
        gather(i + 1, 1 - buf)

    for k in range(TOP_K):
        pltpu.make_async_copy(ys_hbm.at[pl.ds(0, tm)], rows_ref.at[buf, k], sems.at[buf]).wait()

    ff = gate_ref[:, 0:1] * rows_ref[buf, 0]
    for k in range(1, TOP_K):
        ff = ff + gate_ref[:, k:k + 1] * rows_ref[buf, k]
    o_ref[...] = _layer_norm(alpha * x1_ref[...] + ff, lg_ref[...], lb_ref[...])


def _combine(dest_flat, ys, x1, gates_col, ln_g, ln_b, alpha):
    T, D = x1.shape
    tm = min(COMB_TM, T)
    return pl.pallas_call(
        functools.partial(_combine_kernel, alpha, T),
        grid_spec=pltpu.PrefetchScalarGridSpec(
            num_scalar_prefetch=1,
            grid=(T // tm,),
            in_specs=[pl.BlockSpec(memory_space=pl.ANY),
                      pl.BlockSpec((tm, D), lambda i, d: (i, 0)),
                      pl.BlockSpec((tm, TOP_K), lambda i, d: (i, 0)),
                      pl.BlockSpec((1, D), lambda i, d: (0, 0)),
                      pl.BlockSpec((1, D), lambda i, d: (0, 0))],
            out_specs=pl.BlockSpec((tm, D), lambda i, d: (i, 0)),
            scratch_shapes=[pltpu.VMEM((2, TOP_K, tm, D), F32), pltpu.SemaphoreType.DMA((2,))]),
        out_shape=jax.ShapeDtypeStruct((T, D), F32),
        compiler_params=_params(("arbitrary",)),
    )(dest_flat, ys, x1, gates_col, ln_g, ln_b)


def _layer(x2d, B, S, alpha, w_in, b_in, conv_w, conv_b, w_gla_gate, b_gla_gate, m_norm_w, g_norm_w,
           w_branch_m, w_branch_g, w_out, ln1_g, ln1_b, w_router, b_router, w_gu, b_gu, w_down, b_down,
           ln2_g, ln2_b):
    T, D = x2d.shape
    m_end = 2 * M_QKW + 2 * M_VW
    g_beg = m_end + 2 * M_HEADS
    g_end = g_beg + 2 * G_KW + 2 * G_VW
    mg_beg = g_end + G_RANK
    main_cols = [(0, m_end), (g_beg, g_end), (mg_beg, mg_beg + 2 * D)]
    w_main = jnp.concatenate([w_in[:, a:b] for a, b in main_cols], axis=1).astype(BF16)
    b_main = jnp.concatenate([b_in[a:b] for a, b in main_cols])[None, :]
    n_small = 2 * M_HEADS + G_RANK
    w_small = jnp.concatenate([w_in[:, m_end:g_beg], w_in[:, g_end:mg_beg],
                               jnp.zeros((D, SMALL_W - n_small), F32)], axis=1).astype(BF16)
    b_small = jnp.concatenate([b_in[m_end:g_beg], b_in[g_end:mg_beg], jnp.zeros((SMALL_W - n_small,), F32)])[None, :]

    proj, small = _proj(x2d, w_main, b_main, w_small, b_small)
    small_t = small[:, :SUBLANES].T

    ym = _mlstm(proj, small, small_t, conv_w, conv_b[None, :], m_norm_w[None, :], B, S)
    wg_pad = jnp.zeros((SMALL_W, G_KW), F32).at[2 * M_HEADS:n_small].set(w_gla_gate).astype(BF16)
    yg = _gla(proj, small, wg_pad, b_gla_gate[None, :], g_norm_w[None, :], B, S)

    x1, top_e, gates, rank, cnt = _mix(
        ym, yg, proj, x2d, w_branch_m.astype(BF16), w_branch_g.astype(BF16), w_out.astype(BF16),
        ln1_g[None, :], ln1_b[None, :], w_router.T.astype(BF16), b_router[:, None], alpha)

    counts = cnt[:, 0].astype(jnp.int32)
    padded = (counts + EXP_TM - 1) // EXP_TM * EXP_TM
    pend = jnp.cumsum(padded)
    pstart = pend - padded
    n_tiles = (T * TOP_K) // EXP_TM + N_EXPERTS
    tile_start = jnp.arange(n_tiles, dtype=jnp.int32) * EXP_TM
    tile_e = jnp.sum((pend[None, :] <= tile_start[:, None]).astype(jnp.int32), axis=1)
    tile_e = jnp.minimum(tile_e, N_EXPERTS - 1)
    tile_rows = jnp.clip(counts[tile_e] - (tile_start - pstart[tile_e]), 0, EXP_TM)
    n_active = (pend[-1:] // EXP_TM).astype(jnp.int32)
    dest = rank
    for e in range(N_EXPERTS):
        dest = dest + jnp.where(top_e == e, pstart[e], 0)
    dest = dest.reshape(-1)

    xs = _dispatch(dest, x1, n_tiles * EXP_TM)
    n_rows = n_tiles * EXP_TM
    hole_start = jnp.concatenate([pstart + counts, pend[-1:]])
    hole_cnt = jnp.concatenate([padded - counts, n_rows - pend[-1:]])
    xs = _zero_rows(hole_start, hole_cnt, xs)
    ys = _experts(tile_e, tile_rows, n_active, xs, w_gu, b_gu[:, None, :], w_down, b_down[:, None, :])
    return _combine(dest, ys, x1, gates.T, ln2_g[None, :], ln2_b[None, :], alpha)


def kernel(x, w_in, b_in, conv_w, conv_b, w_gla_gate, b_gla_gate, m_norm_w, g_norm_w, w_branch_m, w_branch_g, w_out, ln1_g, ln1_b, w_router, b_router, w_gu, b_gu, w_down, b_down, ln2_g, ln2_b):
    B, S, D = x.shape
    depth = w_in.shape[0]
    alpha = (2 * depth) ** 0.25
    x2d = x.reshape(B * S, D)
    for l in range(depth):
        x2d = _layer(x2d, B, S, alpha, w_in[l], b_in[l], conv_w[l], conv_b[l], w_gla_gate[l], b_gla_gate[l],
                     m_norm_w[l], g_norm_w[l], w_branch_m[l], w_branch_g[l], w_out[l], ln1_g[l], ln1_b[l],
                     w_router[l], b_router[l], w_gu[l], b_gu[l], w_down[l], b_down[l], ln2_g[l], ln2_b[l])
    return x2d.reshape(B, S, D)
```

```python
import functools

import jax
import jax.numpy as jnp
from jax import lax
from jax.experimental import pallas as pl
from jax.experimental.pallas import tpu as pltpu

F32 = jnp.float32
BF16 = jnp.bfloat16

M_HEADS, M_QK, M_V, CONV_W = 4, 128, 256, 4
G_HEADS, G_K, G_V, G_RANK, G_TAU = 4, 128, 256, 16, 16.0
N_EXPERTS, TOP_K = 32, 4
SWIGLU_ALPHA, SWIGLU_LIMIT = 1.702, 7.0
NORM_EPS = 1e-5
M_QKW, M_VW = M_HEADS * M_QK, M_HEADS * M_V
G_KW, G_VW = G_HEADS * G_K, G_HEADS * G_V

LANES = 128
SUBLANES = 8
VMEM_LIMIT = 56 * 1024 * 1024
EXP_VMEM_LIMIT = 60 * 1024 * 1024

PROJ_TM, PROJ_TN = 1024, 1024
SMALL_W = LANES
M_CHUNK = 256
G_STEP = 256
G_CHUNK = 64
MIX_TM = 256
EXP_TM = 512
EXP_TF = 1024
EXP_WPIECE = 512
DISP_TM = 512
COMB_TM = 256
ROW_DMA_UNROLL = 8
ZERO_CHUNK = 64


def _dot(a, b):
    return jnp.dot(a, b, preferred_element_type=F32)


def _dot_nt(a, b):
    return lax.dot_general(a, b, (((1,), (1,)), ((), ())), preferred_element_type=F32)


def _dot_tn(a, b):
    return lax.dot_general(a, b, (((0,), (0,)), ((), ())), preferred_element_type=F32)


def _split3(x):
    hi = x.astype(BF16)
    r = x - hi.astype(F32)
    mid = r.astype(BF16)
    lo = (r - mid.astype(F32)).astype(BF16)
    return hi, mid, lo


def _sigmoid(x):
    return 0.5 * jnp.tanh(0.5 * x) + 0.5


def _log_sigmoid(x):
    return jnp.minimum(x, 0.0) - jnp.log1p(jnp.exp(-jnp.abs(x)))


def _params(sem, vmem_limit=VMEM_LIMIT):
    return pltpu.CompilerParams(dimension_semantics=sem, vmem_limit_bytes=vmem_limit)


def _proj_kernel(x_ref, w_ref, b_ref, ws_ref, bs_ref, o_ref, os_ref, xb_ref):
    @pl.when(pl.program_id(1) == 0)
    def _():
        xb_ref[...] = x_ref[...].astype(BF16)
        os_ref[...] = _dot(xb_ref[...], ws_ref[...]) + bs_ref[...]

    o_ref[...] = _dot(xb_ref[...], w_ref[...]) + b_ref[...]


def _proj(x, w_bf, b, ws_bf, bs):
    T, K = x.shape
    N = w_bf.shape[1]
    tm = min(PROJ_TM, T)
    tn = PROJ_TN
    return pl.pallas_call(
        _proj_kernel,
        grid=(T // tm, N // tn),
        in_specs=[pl.BlockSpec((tm, K), lambda i, j: (i, 0)),
                  pl.BlockSpec((K, tn), lambda i, j: (0, j)),
                  pl.BlockSpec((1, tn), lambda i, j: (0, j)),
                  pl.BlockSpec((K, SMALL_W), lambda i, j: (0, 0)),
                  pl.BlockSpec((1, SMALL_W), lambda i, j: (0, 0))],
        out_specs=[pl.BlockSpec((tm, tn), lambda i, j: (i, j)),
                   pl.BlockSpec((tm, SMALL_W), lambda i, j: (i, 0))],
        out_shape=[jax.ShapeDtypeStruct((T, N), F32), jax.ShapeDtypeStruct((T, SMALL_W), F32)],
        scratch_shapes=[pltpu.VMEM((tm, K), BF16)],
        compiler_params=_params(("parallel", "arbitrary")),
    )(x, w_bf, b, ws_bf, bs)


def _mlstm_kernel(qk_ref, v_ref, og_ref, sm_ref, smt_ref, cw_ref, cb_ref, nw_ref, o_ref,
                  ext_ref, c_ref, n_ref, m_ref):
    L = M_CHUNK
    c = pl.program_id(1)

    @pl.when(c == 0)
    def _():
        ext_ref[0:SUBLANES, :] = jnp.zeros((SUBLANES, 2 * M_QKW), F32)
        c_ref[...] = jnp.zeros_like(c_ref)
        n_ref[...] = jnp.zeros_like(n_ref)
        m_ref[...] = jnp.zeros_like(m_ref)

    ext_ref[SUBLANES:, :] = qk_ref[...]
    acc = cb_ref[...] + cw_ref[CONV_W - 1:CONV_W, :] * ext_ref[SUBLANES:, :]
    for j in range(CONV_W - 1):
        off = SUBLANES - (CONV_W - 1) + j
        acc = acc + cw_ref[j:j + 1, :] * ext_ref[off:off + L, :]
    ext_ref[0:SUBLANES, :] = ext_ref[L:L + SUBLANES, :]
    qk = acc * _sigmoid(acc)

    row = lax.broadcasted_iota(jnp.int32, (L, L), 0)
    col = lax.broadcasted_iota(jnp.int32, (L, L), 1)
    causal = row >= col
    tri = jnp.where(causal, 1.0, 0.0).astype(BF16)
    triu = jnp.where(col >= row, 1.0, 0.0).astype(BF16)

    sm = sm_ref[...]
    smt = smt_ref[...]
    h1, h2, h3 = _split3(_log_sigmoid(sm))
    bcol_all = _dot(tri, h1) + _dot(tri, h2) + _dot(tri, h3)
    r1, r2, r3 = _split3(_log_sigmoid(smt))
    brow_all = _dot(r1, triu) + _dot(r2, triu) + _dot(r3, triu)

    for h in range(M_HEADS):
        q = qk[:, h * M_QK:(h + 1) * M_QK]
        k = qk[:, M_QKW + h * M_QK:M_QKW + (h + 1) * M_QK] * (M_QK ** -0.5)
        v = v_ref[:, h * M_V:(h + 1) * M_V]
        qb, kb, vb = q.astype(BF16), k.astype(BF16), v.astype(BF16)
        li_col = sm[:, h:h + 1]
        b_col = bcol_all[:, M_HEADS + h:M_HEADS + h + 1]
        b_end = b_col[L - 1:L, :]
        li_row = smt[h:h + 1, :]
        b_row = brow_all[M_HEADS + h:M_HEADS + h + 1, :]
        m_prev = m_ref[h:h + 1, 0:1]
        n_prev = n_ref[h:h + 1, :]
        c_prev = c_ref[h]

        d_log = jnp.where(causal, b_col + (li_row - b_row), -jnp.inf)
        m_inter = b_col + m_prev
        m_t = jnp.maximum(m_inter, jnp.max(d_log, axis=-1, keepdims=True))
        dec = jnp.exp(m_inter - m_t)
        s = _dot_nt(qb, kb) * jnp.exp(d_log - m_t)
        num = _dot(s.astype(BF16), vb) + dec * _dot(qb, c_prev.astype(BF16))
        den = jnp.sum(s, axis=-1, keepdims=True) + dec * jnp.sum(q * n_prev, axis=-1, keepdims=True)
        hh = num / jnp.maximum(jnp.abs(den), jnp.exp(-m_t))

        w_end = b_end - b_col + li_col
        g_end = jnp.max(w_end, axis=0, keepdims=True)
        ke = k * jnp.exp(w_end - g_end)
        d_c = _dot_tn(ke.astype(BF16), vb)
        d_n = jnp.sum(ke, axis=0, keepdims=True)
        m_new = jnp.maximum(b_end + m_prev, g_end)
        a = jnp.exp(b_end + m_prev - m_new)
        cc = jnp.exp(g_end - m_new)
        c_ref[h] = a * c_prev + cc * d_c
        n_ref[h:h + 1, :] = a * n_prev + cc * d_n
        m_ref[h:h + 1, :] = jnp.broadcast_to(m_new, (1, LANES))

        mu = jnp.mean(hh, axis=-1, keepdims=True)
        xc = hh - mu
        var = jnp.mean(xc * xc, axis=-1, keepdims=True)
        sl = slice(h * M_V, (h + 1) * M_V)
        y = xc * lax.rsqrt(var + NORM_EPS) * nw_ref[:, sl] * _sigmoid(og_ref[:, sl])
        o_ref[:, sl] = y.astype(o_ref.dtype)


def _mlstm(proj, small, small_t, conv_w, conv_b, norm_w, B, S):
    L = M_CHUNK
    NC = S // L
    T = B * S
    rowblk = lambda b, c: (b * NC + c, 0)
    const = lambda b, c: (0, 0)
    return pl.pallas_call(
        _mlstm_kernel,
        grid=(B, NC),
        in_specs=[pl.BlockSpec((L, 2 * M_QKW), rowblk),
                  pl.BlockSpec((L, M_VW), lambda b, c: (b * NC + c, 1)),
                  pl.BlockSpec((L, M_VW), lambda b, c: (b * NC + c, 2)),
                  pl.BlockSpec((L, SMALL_W), rowblk),
                  pl.BlockSpec((SUBLANES, L), lambda b, c: (0, b * NC + c)),
                  pl.BlockSpec((CONV_W, 2 * M_QKW), const),
                  pl.BlockSpec((1, 2 * M_QKW), const),
                  pl.BlockSpec((1, M_VW), const)],
        out_specs=pl.BlockSpec((L, M_VW), rowblk),
        out_shape=jax.ShapeDtypeStruct((T, M_VW), BF16),
        scratch_shapes=[pltpu.VMEM((L + SUBLANES, 2 * M_QKW), F32),
                        pltpu.VMEM((M_HEADS, M_QK, M_V), F32),
                        pltpu.VMEM((SUBLANES, M_QK), F32),
                        pltpu.VMEM((SUBLANES, LANES), F32)],
        compiler_params=_params(("parallel", "arbitrary")),
    )(proj, proj, proj, small, small_t, conv_w, conv_b, norm_w)


def _gla_kernel(qk_ref, v_ref, gg_ref, sm_ref, wg_ref, bg_ref, nw_ref, o_ref, st_ref):
    LC = G_CHUNK
    c = pl.program_id(1)

    @pl.when(c == 0)
    def _():
        st_ref[...] = jnp.zeros_like(st_ref)

    pre = _dot(sm_ref[...].astype(BF16), wg_ref[...]) + bg_ref[...]
    log_a = _log_sigmoid(pre) / G_TAU

    row = lax.broadcasted_iota(jnp.int32, (LC, LC), 0)
    col = lax.broadcasted_iota(jnp.int32, (LC, LC), 1)
    causal = row >= col
    tri = jnp.where(causal, 1.0, 0.0).astype(BF16)

    for j in range(G_STEP // LC):
        rs = slice(j * LC, (j + 1) * LC)
        a1, a2, a3 = _split3(log_a[rs, :])
        cum = _dot(tri, a1) + _dot(tri, a2) + _dot(tri, a3)
        cum_end = cum[LC - 1:LC, :]
        e_q = jnp.exp(cum)
        e_k = jnp.exp(-cum)
        e_end = jnp.exp(cum_end - cum)
        e_dec = jnp.exp(cum_end)
        for h in range(G_HEADS):
            ks = slice(h * G_K, (h + 1) * G_K)
            vs = slice(h * G_V, (h + 1) * G_V)
            q = qk_ref[rs, ks] * (G_K ** -0.5)
            k = qk_ref[rs, G_KW + h * G_K:G_KW + (h + 1) * G_K]
            vb = v_ref[rs, vs].astype(BF16)
            q_in = (q * e_q[:, ks]).astype(BF16)
            k_in = (k * e_k[:, ks]).astype(BF16)
            k_end = (k * e_end[:, ks]).astype(BF16)
            st = st_ref[h]
            s = jnp.where(causal, _dot_nt(q_in, k_in), 0.0)
            o = _dot(s.astype(BF16), vb) + _dot_nt(q_in, st.astype(BF16))
            st_ref[h] = st * e_dec[:, ks] + _dot_tn(vb, k_end)
            ms = jnp.mean(o * o, axis=-1, keepdims=True)
            g = gg_ref[rs, vs]
            y = o * lax.rsqrt(ms + NORM_EPS) * nw_ref[:, vs] * (g * _sigmoid(g))
            o_ref[rs, vs] = y.astype(o_ref.dtype)


def _gla(proj, small, wg_pad, bg, norm_w, B, S):
    L = G_STEP
    NC = S // L
    T = B * S
    rowblk = lambda b, c: (b * NC + c, 0)
    const = lambda b, c: (0, 0)
    return pl.pallas_call(
        _gla_kernel,
        grid=(B, NC),
        in_specs=[pl.BlockSpec((L, 2 * G_KW), lambda b, c: (b * NC + c, 3)),
                  pl.BlockSpec((L, G_VW), lambda b, c: (b * NC + c, 4)),
                  pl.BlockSpec((L, G_VW), lambda b, c: (b * NC + c, 5)),
                  pl.BlockSpec((L, SMALL_W), rowblk),
                  pl.BlockSpec((SMALL_W, G_KW), const),
                  pl.BlockSpec((1, G_KW), const),
                  pl.BlockSpec((1, G_VW), const)],
        out_specs=pl.BlockSpec((L, G_VW), rowblk),
        out_shape=jax.ShapeDtypeStruct((T, G_VW), BF16),
        scratch_shapes=[pltpu.VMEM((G_HEADS, G_V, G_K), F32)],
        compiler_params=_params(("parallel", "arbitrary")),
    )(proj, proj, proj, small, wg_pad, bg, norm_w)


def _layer_norm(r, g, b):
    mu = jnp.mean(r, axis=-1, keepdims=True)
    xc = r - mu
    var = jnp.mean(xc * xc, axis=-1, keepdims=True)
    return xc * lax.rsqrt(var + NORM_EPS) * g + b


def _mix_kernel(alpha, ym_ref, yg_ref, gm_ref, gg_ref, x_ref, wbm_ref, wbg_ref, wout_ref, lg_ref, lb_ref,
                wr_ref, br_ref, x1_ref, tope_ref, gate_ref, rank_ref, cnt_ref, carry_ref):
    tm = x_ref.shape[0]
    i = pl.program_id(0)

    @pl.when(i == 0)
    def _():
        carry_ref[...] = jnp.zeros_like(carry_ref)

    pm = _dot(ym_ref[...], wbm_ref[...])
    pg = _dot(yg_ref[...], wbg_ref[...])
    z = _sigmoid(gm_ref[...]) * pm + _sigmoid(gg_ref[...]) * pg
    mix = _dot(z.astype(BF16), wout_ref[...])
    x1 = _layer_norm(alpha * x_ref[...] + mix, lg_ref[...], lb_ref[...])
    x1_ref[...] = x1

    logits = _dot_nt(wr_ref[...], x1.astype(BF16)) + br_ref[...]
    eidx = lax.broadcasted_iota(jnp.int32, (N_EXPERTS, tm), 0)
    vals, hots = [], []
    cur = logits
    for k in range(TOP_K):
        mx = jnp.max(cur, axis=0, keepdims=True)
        idx = jnp.min(jnp.where(cur == mx, eidx, N_EXPERTS), axis=0, keepdims=True)
        hot = eidx == idx
        cur = jnp.where(hot, -jnp.inf, cur)
        vals.append(mx)
        hots.append(hot)
        tope_ref[k:k + 1, :] = idx
    exps = [jnp.exp(v - vals[0]) for v in vals]
    tot = exps[0] + exps[1] + exps[2] + exps[3]
    for k in range(TOP_K):
        gate_ref[k:k + 1, :] = exps[k] / tot

    sel = jnp.zeros((N_EXPERTS, tm), F32)
    for hot in hots:
        sel = sel + jnp.where(hot, 1.0, 0.0)
    row = lax.broadcasted_iota(jnp.int32, (tm, tm), 0)
    col = lax.broadcasted_iota(jnp.int32, (tm, tm), 1)
    before = jnp.where(row < col, 1.0, 0.0).astype(BF16)
    pos = _dot(sel.astype(BF16), before) + carry_ref[:, 0:1]
    for k in range(TOP_K):
        rk = jnp.sum(jnp.where(hots[k], pos, 0.0), axis=0, keepdims=True)
        rank_ref[k:k + 1, :] = rk.astype(jnp.int32)
    carry_ref[...] = carry_ref[...] + jnp.sum(sel, axis=1, keepdims=True)
    cnt_ref[...] = carry_ref[...]


def _mix(ym, yg, proj, x2d, wbm, wbg, wout, ln_g, ln_b, wr_t, br, alpha):
    T, D = x2d.shape
    tm = min(MIX_TM, T)
    rowblk = lambda i: (i, 0)
    const = lambda i: (0, 0)
    once = dict(pipeline_mode=pl.Buffered(1))
    return pl.pallas_call(
        functools.partial(_mix_kernel, alpha),
        grid=(T // tm,),
        in_specs=[pl.BlockSpec((tm, M_VW), rowblk),
                  pl.BlockSpec((tm, G_VW), rowblk),
                  pl.BlockSpec((tm, D), lambda i: (i, 3)),
                  pl.BlockSpec((tm, D), lambda i: (i, 4)),
                  pl.BlockSpec((tm, D), rowblk),
                  pl.BlockSpec((M_VW, D), const, **once),
                  pl.BlockSpec((G_VW, D), const, **once),
                  pl.BlockSpec((D, D), const, **once),
                  pl.BlockSpec((1, D), const),
                  pl.BlockSpec((1, D), const),
                  pl.BlockSpec((N_EXPERTS, D), const),
                  pl.BlockSpec((N_EXPERTS, 1), const)],
        out_specs=[pl.BlockSpec((tm, D), rowblk),
                   pl.BlockSpec((TOP_K, tm), lambda i: (0, i)),
                   pl.BlockSpec((TOP_K, tm), lambda i: (0, i)),
                   pl.BlockSpec((TOP_K, tm), lambda i: (0, i)),
                   pl.BlockSpec((N_EXPERTS, LANES), const)],
        out_shape=[jax.ShapeDtypeStruct((T, D), F32),
                   jax.ShapeDtypeStruct((TOP_K, T), jnp.int32),
                   jax.ShapeDtypeStruct((TOP_K, T), F32),
                   jax.ShapeDtypeStruct((TOP_K, T), jnp.int32),
                   jax.ShapeDtypeStruct((N_EXPERTS, LANES), F32)],
        scratch_shapes=[pltpu.VMEM((N_EXPERTS, LANES), F32)],
        compiler_params=_params(("arbitrary",)),
    )(ym, yg, proj, proj, x2d, wbm, wbg, wout, ln_g, ln_b, wr_t, br)


def _row_copy(src, src_row, dst, dst_row, sem):
    return pltpu.make_async_copy(src.at[pl.ds(src_row, 1)], dst.at[pl.ds(dst_row, 1)], sem)


def _dispatch_kernel(n_tok, dest_ref, x1_hbm, xs_hbm, buf_ref, load_sems, row_sems):
    tm = buf_ref.shape[1]
    i = pl.program_id(0)
    n = pl.num_programs(0)

    def load(tile):
        slot = tile % 3
        return pltpu.make_async_copy(x1_hbm.at[pl.ds(pl.multiple_of(tile * tm, tm), tm)], buf_ref.at[slot],
                                     load_sems.at[slot])

    def wait_rows(tile):
        for k in range(TOP_K):
            pltpu.make_async_copy(buf_ref.at[tile % 3], xs_hbm.at[pl.ds(0, tm)], row_sems.at[tile % 2]).wait()

    @pl.when(i == 0)
    def _():
        load(0).start()

    @pl.when(i + 1 < n)
    def _():
        load(i + 1).start()

    load(i).wait()
    src = buf_ref.at[i % 3]
    base = i * tm

    def issue(t, carry):
        for k in range(TOP_K):
            _row_copy(src, t, xs_hbm, dest_ref[k * n_tok + base + t], row_sems.at[i % 2]).start()
        return carry

    lax.fori_loop(0, tm, issue, 0, unroll=ROW_DMA_UNROLL)

    @pl.when(i > 0)
    def _():
        wait_rows(i - 1)

    @pl.when(i == n - 1)
    def _():
        wait_rows(i)


def _dispatch(dest_flat, x1, n_rows):
    T, D = x1.shape
    tm = min(DISP_TM, T)
    return pl.pallas_call(
        functools.partial(_dispatch_kernel, T),
        grid_spec=pltpu.PrefetchScalarGridSpec(
            num_scalar_prefetch=1,
            grid=(T // tm,),
            in_specs=[pl.BlockSpec(memory_space=pl.ANY)],
            out_specs=pl.BlockSpec(memory_space=pl.ANY),
            scratch_shapes=[pltpu.VMEM((3, tm, D), x1.dtype),
                            pltpu.SemaphoreType.DMA((3,)),
                            pltpu.SemaphoreType.DMA((2,))]),
        out_shape=jax.ShapeDtypeStruct((n_rows, D), x1.dtype),
        compiler_params=pltpu.CompilerParams(dimension_semantics=("arbitrary",), has_side_effects=True),
    )(dest_flat, x1)


def _zero_rows_kernel(start_ref, cnt_ref, xs_in_hbm, xs_hbm, zrow_ref, sem):
    del xs_in_hbm
    zrow_ref[...] = jnp.zeros_like(zrow_ref)

    def per_range(e, carry):
        start, cnt = start_ref[e], cnt_ref[e]
        head = jnp.minimum(cnt, (-start) & (SUBLANES - 1))
        mid = start + head
        n_blk = (cnt - head) // ZERO_CHUNK
        tail = mid + n_blk * ZERO_CHUNK
        n_tail = cnt - head - n_blk * ZERO_CHUNK

        def blk_copy(r):
            dst = pl.multiple_of(mid + r * ZERO_CHUNK, SUBLANES)
            return pltpu.make_async_copy(zrow_ref, xs_hbm.at[pl.ds(dst, ZERO_CHUNK)], sem)

        def rows(first, n):
            def issue(r, c):
                _row_copy(zrow_ref, 0, xs_hbm, first + r, sem).start()
                return c

            def drain(r, c):
                _row_copy(zrow_ref, 0, xs_hbm, 0, sem).wait()
                return c

            lax.fori_loop(0, n, issue, 0)
            lax.fori_loop(0, n, drain, 0)

        def issue_blk(r, c):
            blk_copy(r).start()
            return c

        def drain_blk(r, c):
            blk_copy(0).wait()
            return c

        rows(start, head)
        lax.fori_loop(0, n_blk, issue_blk, 0)
        lax.fori_loop(0, n_blk, drain_blk, 0)
        rows(tail, n_tail)
        return carry

    lax.fori_loop(0, start_ref.shape[0], per_range, 0)


def _zero_rows(start, cnt, xs):
    return pl.pallas_call(
        _zero_rows_kernel,
        grid_spec=pltpu.PrefetchScalarGridSpec(
            num_scalar_prefetch=2,
            grid=(1,),
            in_specs=[pl.BlockSpec(memory_space=pl.ANY)],
            out_specs=pl.BlockSpec(memory_space=pl.ANY),
            scratch_shapes=[pltpu.VMEM((ZERO_CHUNK, xs.shape[1]), xs.dtype), pltpu.SemaphoreType.DMA(())]),
        out_shape=jax.ShapeDtypeStruct(xs.shape, xs.dtype),
        input_output_aliases={2: 0},
        compiler_params=pltpu.CompilerParams(dimension_semantics=("arbitrary",), has_side_effects=True),
    )(start, cnt, xs)


def _expert_kernel(te_ref, tr_ref, na_ref, xs_ref, wgu_hbm, bgu_ref, wd_hbm, bd_ref, o_ref,
                   wgu_ref, wd_ref, stage_ref, sems):
    i = pl.program_id(0)
    fh, d = wd_ref.shape
    active = i < na_ref[0]
    e = te_ref[i]
    changed = jnp.logical_or(i == 0, e != te_ref[jnp.maximum(i - 1, 0)])

    def swiglu(a, tf, f):
        glu = slice(f * tf, (f + 1) * tf)
        lin = slice(fh + f * tf, fh + (f + 1) * tf)
        g = _dot(a, wgu_ref[:, glu]) + bgu_ref[:, glu]
        l = _dot(a, wgu_ref[:, lin]) + bgu_ref[:, lin]
        a_glu = jnp.minimum(g, SWIGLU_LIMIT)
        a_lin = jnp.clip(l, -SWIGLU_LIMIT, SWIGLU_LIMIT)
        return (a_glu * _sigmoid(SWIGLU_ALPHA * a_glu) * (a_lin + 1.0)).astype(BF16)

    @pl.when(jnp.logical_and(active, changed))
    def _():
        ts = stage_ref.shape[2]
        nf = fh // ts
        n_piece = 3 * nf

        def copies(p):
            b = p % 2
            if p < 2 * nf:
                f, lin = divmod(p, 2)
                c0 = lin * fh + f * ts
                return [pltpu.make_async_copy(wgu_hbm.at[e, :, pl.ds(c0, ts)], stage_ref.at[b], sems.at[b])]
            r0 = (p - 2 * nf) * ts
            return [pltpu.make_async_copy(wd_hbm.at[e, pl.ds(r0, ts), pl.ds(j * ts, ts)],
                                          stage_ref.at[b, pl.ds(j * ts, ts), :], sems.at[b])
                    for j in range(d // ts)]

        def land(p):
            for cp in copies(p):
                cp.wait()
            b = p % 2
            if p < 2 * nf:
                f, lin = divmod(p, 2)
                c0 = lin * fh + f * ts
                wgu_ref[:, c0:c0 + ts] = stage_ref[b].astype(BF16)
            else:
                r0 = (p - 2 * nf) * ts
                for j in range(d // ts):
                    wd_ref[r0:r0 + ts, j * ts:(j + 1) * ts] = stage_ref[b, j * ts:(j + 1) * ts, :].astype(BF16)

        for p in range(2):
            for cp in copies(p):
                cp.start()
        a = xs_ref[...].astype(BF16)
        acts, acc = [], None
        for p in range(n_piece):
            land(p)
            if p + 2 < n_piece:
                for cp in copies(p + 2):
                    cp.start()
            if p < 2 * nf and p % 2 == 1:
                acts.append(swiglu(a, ts, p // 2))
            if p >= 2 * nf:
                f = p - 2 * nf
                part = _dot(acts[f], wd_ref[f * ts:(f + 1) * ts, :])
                acc = part if acc is None else acc + part
        o_ref[...] = acc + bd_ref[...]

    def ffn(a):
        tf = min(EXP_TF, fh)
        acc = None
        for f in range(fh // tf):
            part = _dot(swiglu(a, tf, f), wd_ref[f * tf:(f + 1) * tf, :])
            acc = part if acc is None else acc + part
        return acc + bd_ref[...]

    half = xs_ref.shape[0] // 2
    later = jnp.logical_and(active, jnp.logical_not(changed))
    short = tr_ref[i] <= half

    @pl.when(jnp.logical_and(later, jnp.logical_not(short)))
    def _():
        o_ref[...] = ffn(xs_ref[...].astype(BF16))

    @pl.when(jnp.logical_and(later, short))
    def _():
        o_ref[0:half, :] = ffn(xs_ref[0:half, :].astype(BF16))
        o_ref[half:, :] = jnp.zeros((half, d), F32)

    @pl.when(i >= na_ref[0])
    def _():
        o_ref[...] = jnp.zeros_like(o_ref)


def _experts(tile_e, tile_rows, n_active, xs, w_gu, b_gu, w_down, b_down):
    P, D = xs.shape
    E, _, F2 = w_gu.shape
    Fh = F2 // 2
    n_tiles = P // EXP_TM

    def tile(i, na):
        return jnp.minimum(i, na[0] - 1)

    return pl.pallas_call(
        _expert_kernel,
        grid_spec=pltpu.PrefetchScalarGridSpec(
            num_scalar_prefetch=3,
            grid=(n_tiles,),
            in_specs=[
                pl.BlockSpec((EXP_TM, D), lambda i, te, tr, na: (tile(i, na), 0)),
                pl.BlockSpec(memory_space=pl.ANY),
                pl.BlockSpec((None, 1, F2), lambda i, te, tr, na: (te[tile(i, na)], 0, 0)),
                pl.BlockSpec(memory_space=pl.ANY),
                pl.BlockSpec((None, 1, D), lambda i, te, tr, na: (te[tile(i, na)], 0, 0)),
            ],
            out_specs=pl.BlockSpec((EXP_TM, D), lambda i, te, tr, na: (i, 0)),
            scratch_shapes=[pltpu.VMEM((D, F2), BF16),
                            pltpu.VMEM((Fh, D), BF16),
                            pltpu.VMEM((2, D, EXP_WPIECE), F32),
                            pltpu.SemaphoreType.DMA((2,))]),
        out_shape=jax.ShapeDtypeStruct((P, D), F32),
        compiler_params=_params(("arbitrary",), EXP_VMEM_LIMIT),
    )(tile_e, tile_rows, n_active, xs, w_gu, b_gu, w_down, b_down)


def _combine_kernel(alpha, n_tok, dest_ref, ys_hbm, x1_ref, gate_ref, lg_ref, lb_ref, o_ref, rows_ref, sems):
    tm = x1_ref.shape[0]
    i = pl.program_id(0)
    buf = i % 2

    def gather(tile, b):
        base = tile * tm

        def issue(t, carry):
            for k in range(TOP_K):
                _row_copy(ys_hbm, dest_ref[k * n_tok + base + t], rows_ref.at[b, k], t, sems.at[b]).start()
            return carry

        lax.fori_loop(0, tm, issue, 0, unroll=ROW_DMA_UNROLL)

    @pl.when(i == 0)
    def _():
        gather(0, 0)

    @pl.when(i + 1 < pl.num_programs(0))
    def _():
        gather(i + 1, 1 - buf)

    for k in range(TOP_K):
        pltpu.make_async_copy(ys_hbm.at[pl.ds(0, tm)], rows_ref.at[buf, k], sems.at[buf]).wait()

    ff = gate_ref[:, 0:1] * rows_ref[buf, 0]
    for k in range(1, TOP_K):
        ff = ff + gate_ref[:, k:k + 1] * rows_ref[buf, k]
    o_ref[...] = _layer_norm(alpha * x1_ref[...] + ff, lg_ref[...], lb_ref[...])


def _combine(dest_flat, ys, x1, gates_col, ln_g, ln_b, alpha):
    T, D = x1.shape
    tm = min(COMB_TM, T)
    return pl.pallas_call(
        functools.partial(_combine_kernel, alpha, T),
        grid_spec=pltpu.PrefetchScalarGridSpec(
            num_scalar_prefetch=1,
            grid=(T // tm,),
            in_specs=[pl.BlockSpec(memory_space=pl.ANY),
                      pl.BlockSpec((tm, D), lambda i, d: (i, 0)),
                      pl.BlockSpec((tm, TOP_K), lambda i, d: (i, 0)),
                      pl.BlockSpec((1, D), lambda i, d: (0, 0)),
                      pl.BlockSpec((1, D), lambda i, d: (0, 0))],
            out_specs=pl.BlockSpec((tm, D), lambda i, d: (i, 0)),
            scratch_shapes=[pltpu.VMEM((2, TOP_K, tm, D), F32), pltpu.SemaphoreType.DMA((2,))]),
        out_shape=jax.ShapeDtypeStruct((T, D), F32),
        compiler_params=_params(("arbitrary",)),
    )(dest_flat, ys, x1, gates_col, ln_g, ln_b)


def _layer(x2d, B, S, alpha, w_in, b_in, conv_w, conv_b, w_gla_gate, b_gla_gate, m_norm_w, g_norm_w,
           w_branch_m, w_branch_g, w_out, ln1_g, ln1_b, w_router, b_router, w_gu, b_gu, w_down, b_down,
           ln2_g, ln2_b):
    T, D = x2d.shape
    m_end = 2 * M_QKW + 2 * M_VW
    g_beg = m_end + 2 * M_HEADS
    g_end = g_beg + 2 * G_KW + 2 * G_VW
    mg_beg = g_end + G_RANK
    main_cols = [(0, m_end), (g_beg, g_end), (mg_beg, mg_beg + 2 * D)]
    w_main = jnp.concatenate([w_in[:, a:b] for a, b in main_cols], axis=1).astype(BF16)
    b_main = jnp.concatenate([b_in[a:b] for a, b in main_cols])[None, :]
    n_small = 2 * M_HEADS + G_RANK
    w_small = jnp.concatenate([w_in[:, m_end:g_beg], w_in[:, g_end:mg_beg],
                               jnp.zeros((D, SMALL_W - n_small), F32)], axis=1).astype(BF16)
    b_small = jnp.concatenate([b_in[m_end:g_beg], b_in[g_end:mg_beg], jnp.zeros((SMALL_W - n_small,), F32)])[None, :]

    proj, small = _proj(x2d, w_main, b_main, w_small, b_small)
    small_t = small[:, :SUBLANES].T

    ym = _mlstm(proj, small, small_t, conv_w, conv_b[None, :], m_norm_w[None, :], B, S)
    wg_pad = jnp.zeros((SMALL_W, G_KW), F32).at[2 * M_HEADS:n_small].set(w_gla_gate).astype(BF16)
    yg = _gla(proj, small, wg_pad, b_gla_gate[None, :], g_norm_w[None, :], B, S)

    x1, top_e, gates, rank, cnt = _mix(
        ym, yg, proj, x2d, w_branch_m.astype(BF16), w_branch_g.astype(BF16), w_out.astype(BF16),
        ln1_g[None, :], ln1_b[None, :], w_router.T.astype(BF16), b_router[:, None], alpha)

    counts = cnt[:, 0].astype(jnp.int32)
    padded = (counts + EXP_TM - 1) // EXP_TM * EXP_TM
    pend = jnp.cumsum(padded)
    pstart = pend - padded
    n_tiles = (T * TOP_K) // EXP_TM + N_EXPERTS
    tile_start = jnp.arange(n_tiles, dtype=jnp.int32) * EXP_TM
    tile_e = jnp.sum((pend[None, :] <= tile_start[:, None]).astype(jnp.int32), axis=1)
    tile_e = jnp.minimum(tile_e, N_EXPERTS - 1)
    tile_rows = jnp.clip(counts[tile_e] - (tile_start - pstart[tile_e]), 0, EXP_TM)
    n_active = (pend[-1:] // EXP_TM).astype(jnp.int32)
    dest = rank
    for e in range(N_EXPERTS):
        dest = dest + jnp.where(top_e == e, pstart[e], 0)
    dest = dest.reshape(-1)

    xs = _dispatch(dest, x1, n_tiles * EXP_TM)
    n_rows = n_tiles * EXP_TM
    hole_start = jnp.concatenate([pstart + counts, pend[-1:]])
    hole_cnt = jnp.concatenate([padded - counts, n_rows - pend[-1:]])
    xs = _zero_rows(hole_start, hole_cnt, xs)
    ys = _experts(tile_e, tile_rows, n_active, xs, w_gu, b_gu[:, None, :], w_down, b_down[:, None, :])
    return _combine(dest, ys, x1, gates.T, ln2_g[None, :], ln2_b[None, :], alpha)


def kernel(x, w_in, b_in, conv_w, conv_b, w_gla_gate, b_gla_gate, m_norm_w, g_norm_w, w_branch_m, w_branch_g, w_out, ln1_g, ln1_b, w_router, b_router, w_gu, b_gu, w_down, b_down, ln2_g, ln2_b):
    B, S, D = x.shape
    depth = w_in.shape[0]
    alpha = (2 * depth) ** 0.25
    x2d = x.reshape(B * S, D)
    for l in range(depth):
        x2d = _layer(x2d, B, S, alpha, w_in[l], b_in[l], conv_w[l], conv_b[l], w_gla_gate[l], b_gla_gate[l],
                     m_norm_w[l], g_norm_w[l], w_branch_m[l], w_branch_g[l], w_out[l], ln1_g[l], ln1_b[l],
                     w_router[l], b_router[l], w_gu[l], b_gu[l], w_down[l], b_down[l], ln2_g[l], ln2_b[l])
    return x2d.reshape(B, S, D)
```

```python
import functools

import jax
import jax.numpy as jnp
from jax import lax
from jax.experimental import pallas as pl
from jax.experimental.pallas import tpu as pltpu

F32 = jnp.float32
BF16 = jnp.bfloat16

M_HEADS, M_QK, M_V, CONV_W = 4, 128, 256, 4
G_HEADS, G_K, G_V, G_RANK, G_TAU = 4, 128, 256, 16, 16.0
N_EXPERTS, TOP_K = 32, 4
SWIGLU_ALPHA, SWIGLU_LIMIT = 1.702, 7.0
NORM_EPS = 1e-5
M_QKW, M_VW = M_HEADS * M_QK, M_HEADS * M_V
G_KW, G_VW = G_HEADS * G_K, G_HEADS * G_V

LANES = 128
SUBLANES = 8
VMEM_LIMIT = 56 * 1024 * 1024
EXP_VMEM_LIMIT = 60 * 1024 * 1024

PROJ_TM, PROJ_TN = 1024, 1024
SMALL_W = LANES
M_CHUNK = 256
G_STEP = 256
G_CHUNK = 64
MIX_TM = 256
EXP_TM = 512
EXP_TF = 1024
EXP_WPIECE = 512
DISP_TM = 512
COMB_TM = 256
COMB_GROUPS = 8
ROW_DMA_UNROLL = 8
ZERO_CHUNK = 64


def _dot(a, b):
    return jnp.dot(a, b, preferred_element_type=F32)


def _dot_nt(a, b):
    return lax.dot_general(a, b, (((1,), (1,)), ((), ())), preferred_element_type=F32)


def _dot_tn(a, b):
    return lax.dot_general(a, b, (((0,), (0,)), ((), ())), preferred_element_type=F32)


def _split3(x):
    hi = x.astype(BF16)
    r = x - hi.astype(F32)
    mid = r.astype(BF16)
    lo = (r - mid.astype(F32)).astype(BF16)
    return hi, mid, lo


def _sigmoid(x):
    return 0.5 * jnp.tanh(0.5 * x) + 0.5


def _log_sigmoid(x):
    return jnp.minimum(x, 0.0) - jnp.log1p(jnp.exp(-jnp.abs(x)))


def _params(sem, vmem_limit=VMEM_LIMIT):
    return pltpu.CompilerParams(dimension_semantics=sem, vmem_limit_bytes=vmem_limit)


def _proj_kernel(x_ref, w_ref, b_ref, ws_ref, bs_ref, o_ref, os_ref, xb_ref):
    @pl.when(pl.program_id(1) == 0)
    def _():
        xb_ref[...] = x_ref[...].astype(BF16)
        os_ref[...] = _dot(xb_ref[...], ws_ref[...]) + bs_ref[...]

    o_ref[...] = _dot(xb_ref[...], w_ref[...]) + b_ref[...]


def _proj(x, w_bf, b, ws_bf, bs):
    T, K = x.shape
    N = w_bf.shape[1]
    tm = min(PROJ_TM, T)
    tn = PROJ_TN
    return pl.pallas_call(
        _proj_kernel,
        grid=(T // tm, N // tn),
        in_specs=[pl.BlockSpec((tm, K), lambda i, j: (i, 0)),
                  pl.BlockSpec((K, tn), lambda i, j: (0, j)),
                  pl.BlockSpec((1, tn), lambda i, j: (0, j)),
                  pl.BlockSpec((K, SMALL_W), lambda i, j: (0, 0)),
                  pl.BlockSpec((1, SMALL_W), lambda i, j: (0, 0))],
        out_specs=[pl.BlockSpec((tm, tn), lambda i, j: (i, j)),
                   pl.BlockSpec((tm, SMALL_W), lambda i, j: (i, 0))],
        out_shape=[jax.ShapeDtypeStruct((T, N), F32), jax.ShapeDtypeStruct((T, SMALL_W), F32)],
        scratch_shapes=[pltpu.VMEM((tm, K), BF16)],
        compiler_params=_params(("parallel", "arbitrary")),
    )(x, w_bf, b, ws_bf, bs)


def _mlstm_kernel(qk_ref, v_ref, og_ref, sm_ref, smt_ref, cw_ref, cb_ref, nw_ref, o_ref,
                  ext_ref, c_ref, n_ref, m_ref):
    L = M_CHUNK
    c = pl.program_id(1)

    @pl.when(c == 0)
    def _():
        ext_ref[0:SUBLANES, :] = jnp.zeros((SUBLANES, 2 * M_QKW), F32)
        c_ref[...] = jnp.zeros_like(c_ref)
        n_ref[...] = jnp.zeros_like(n_ref)
        m_ref[...] = jnp.zeros_like(m_ref)

    ext_ref[SUBLANES:, :] = qk_ref[...]
    acc = cb_ref[...] + cw_ref[CONV_W - 1:CONV_W, :] * ext_ref[SUBLANES:, :]
    for j in range(CONV_W - 1):
        off = SUBLANES - (CONV_W - 1) + j
        acc = acc + cw_ref[j:j + 1, :] * ext_ref[off:off + L, :]
    ext_ref[0:SUBLANES, :] = ext_ref[L:L + SUBLANES, :]
    qk = acc * _sigmoid(acc)

    row = lax.broadcasted_iota(jnp.int32, (L, L), 0)
    col = lax.broadcasted_iota(jnp.int32, (L, L), 1)
    causal = row >= col
    tri = jnp.where(causal, 1.0, 0.0).astype(BF16)
    triu = jnp.where(col >= row, 1.0, 0.0).astype(BF16)

    sm = sm_ref[...]
    smt = smt_ref[...]
    h1, h2, h3 = _split3(_log_sigmoid(sm))
    bcol_all = _dot(tri, h1) + _dot(tri, h2) + _dot(tri, h3)
    r1, r2, r3 = _split3(_log_sigmoid(smt))
    brow_all = _dot(r1, triu) + _dot(r2, triu) + _dot(r3, triu)

    for h in range(M_HEADS):
        q = qk[:, h * M_QK:(h + 1) * M_QK]
        k = qk[:, M_QKW + h * M_QK:M_QKW + (h + 1) * M_QK] * (M_QK ** -0.5)
        v = v_ref[:, h * M_V:(h + 1) * M_V]
        qb, kb, vb = q.astype(BF16), k.astype(BF16), v.astype(BF16)
        li_col = sm[:, h:h + 1]
        b_col = bcol_all[:, M_HEADS + h:M_HEADS + h + 1]
        b_end = b_col[L - 1:L, :]
        li_row = smt[h:h + 1, :]
        b_row = brow_all[M_HEADS + h:M_HEADS + h + 1, :]
        m_prev = m_ref[h:h + 1, 0:1]
        n_prev = n_ref[h:h + 1, :]
        c_prev = c_ref[h]

        d_log = jnp.where(causal, b_col + (li_row - b_row), -jnp.inf)
        m_inter = b_col + m_prev
        m_t = jnp.maximum(m_inter, jnp.max(d_log, axis=-1, keepdims=True))
        dec = jnp.exp(m_inter - m_t)
        s = _dot_nt(qb, kb) * jnp.exp(d_log - m_t)
        num = _dot(s.astype(BF16), vb) + dec * _dot(qb, c_prev.astype(BF16))
        den = jnp.sum(s, axis=-1, keepdims=True) + dec * jnp.sum(q * n_prev, axis=-1, keepdims=True)
        hh = num / jnp.maximum(jnp.abs(den), jnp.exp(-m_t))

        w_end = b_end - b_col + li_col
        g_end = jnp.max(w_end, axis=0, keepdims=True)
        ke = k * jnp.exp(w_end - g_end)
        d_c = _dot_tn(ke.astype(BF16), vb)
        d_n = jnp.sum(ke, axis=0, keepdims=True)
        m_new = jnp.maximum(b_end + m_prev, g_end)
        a = jnp.exp(b_end + m_prev - m_new)
        cc = jnp.exp(g_end - m_new)
        c_ref[h] = a * c_prev + cc * d_c
        n_ref[h:h + 1, :] = a * n_prev + cc * d_n
        m_ref[h:h + 1, :] = jnp.broadcast_to(m_new, (1, LANES))

        mu = jnp.mean(hh, axis=-1, keepdims=True)
        xc = hh - mu
        var = jnp.mean(xc * xc, axis=-1, keepdims=True)
        sl = slice(h * M_V, (h + 1) * M_V)
        y = xc * lax.rsqrt(var + NORM_EPS) * nw_ref[:, sl] * _sigmoid(og_ref[:, sl])
        o_ref[:, sl] = y.astype(o_ref.dtype)


def _mlstm(proj, small, small_t, conv_w, conv_b, norm_w, B, S):
    L = M_CHUNK
    NC = S // L
    T = B * S
    rowblk = lambda b, c: (b * NC + c, 0)
    const = lambda b, c: (0, 0)
    return pl.pallas_call(
        _mlstm_kernel,
        grid=(B, NC),
        in_specs=[pl.BlockSpec((L, 2 * M_QKW), rowblk),
                  pl.BlockSpec((L, M_VW), lambda b, c: (b * NC + c, 1)),
                  pl.BlockSpec((L, M_VW), lambda b, c: (b * NC + c, 2)),
                  pl.BlockSpec((L, SMALL_W), rowblk),
                  pl.BlockSpec((SUBLANES, L), lambda b, c: (0, b * NC + c)),
                  pl.BlockSpec((CONV_W, 2 * M_QKW), const),
                  pl.BlockSpec((1, 2 * M_QKW), const),
                  pl.BlockSpec((1, M_VW), const)],
        out_specs=pl.BlockSpec((L, M_VW), rowblk),
        out_shape=jax.ShapeDtypeStruct((T, M_VW), BF16),
        scratch_shapes=[pltpu.VMEM((L + SUBLANES, 2 * M_QKW), F32),
                        pltpu.VMEM((M_HEADS, M_QK, M_V), F32),
                        pltpu.VMEM((SUBLANES, M_QK), F32),
                        pltpu.VMEM((SUBLANES, LANES), F32)],
        compiler_params=_params(("parallel", "arbitrary")),
    )(proj, proj, proj, small, small_t, conv_w, conv_b, norm_w)


def _gla_kernel(qk_ref, v_ref, gg_ref, sm_ref, wg_ref, bg_ref, nw_ref, o_ref, st_ref):
    LC = G_CHUNK
    c = pl.program_id(1)

    @pl.when(c == 0)
    def _():
        st_ref[...] = jnp.zeros_like(st_ref)

    pre = _dot(sm_ref[...].astype(BF16), wg_ref[...]) + bg_ref[...]
    log_a = _log_sigmoid(pre) / G_TAU

    row = lax.broadcasted_iota(jnp.int32, (LC, LC), 0)
    col = lax.broadcasted_iota(jnp.int32, (LC, LC), 1)
    causal = row >= col
    tri = jnp.where(causal, 1.0, 0.0).astype(BF16)

    for j in range(G_STEP // LC):
        rs = slice(j * LC, (j + 1) * LC)
        a1, a2, a3 = _split3(log_a[rs, :])
        cum = _dot(tri, a1) + _dot(tri, a2) + _dot(tri, a3)
        cum_end = cum[LC - 1:LC, :]
        e_q = jnp.exp(cum)
        e_k = jnp.exp(-cum)
        e_end = jnp.exp(cum_end - cum)
        e_dec = jnp.exp(cum_end)
        for h in range(G_HEADS):
            ks = slice(h * G_K, (h + 1) * G_K)
            vs = slice(h * G_V, (h + 1) * G_V)
            q = qk_ref[rs, ks] * (G_K ** -0.5)
            k = qk_ref[rs, G_KW + h * G_K:G_KW + (h + 1) * G_K]
            vb = v_ref[rs, vs].astype(BF16)
            q_in = (q * e_q[:, ks]).astype(BF16)
            k_in = (k * e_k[:, ks]).astype(BF16)
            k_end = (k * e_end[:, ks]).astype(BF16)
            st = st_ref[h]
            s = jnp.where(causal, _dot_nt(q_in, k_in), 0.0)
            o = _dot(s.astype(BF16), vb) + _dot_nt(q_in, st.astype(BF16))
            st_ref[h] = st * e_dec[:, ks] + _dot_tn(vb, k_end)
            ms = jnp.mean(o * o, axis=-1, keepdims=True)
            g = gg_ref[rs, vs]
            y = o * lax.rsqrt(ms + NORM_EPS) * nw_ref[:, vs] * (g * _sigmoid(g))
            o_ref[rs, vs] = y.astype(o_ref.dtype)


def _gla(proj, small, wg_pad, bg, norm_w, B, S):
    L = G_STEP
    NC = S // L
    T = B * S
    rowblk = lambda b, c: (b * NC + c, 0)
    const = lambda b, c: (0, 0)
    return pl.pallas_call(
        _gla_kernel,
        grid=(B, NC),
        in_specs=[pl.BlockSpec((L, 2 * G_KW), lambda b, c: (b * NC + c, 3)),
                  pl.BlockSpec((L, G_VW), lambda b, c: (b * NC + c, 4)),
                  pl.BlockSpec((L, G_VW), lambda b, c: (b * NC + c, 5)),
                  pl.BlockSpec((L, SMALL_W), rowblk),
                  pl.BlockSpec((SMALL_W, G_KW), const),
                  pl.BlockSpec((1, G_KW), const),
                  pl.BlockSpec((1, G_VW), const)],
        out_specs=pl.BlockSpec((L, G_VW), rowblk),
        out_shape=jax.ShapeDtypeStruct((T, G_VW), BF16),
        scratch_shapes=[pltpu.VMEM((G_HEADS, G_V, G_K), F32)],
        compiler_params=_params(("parallel", "arbitrary")),
    )(proj, proj, proj, small, wg_pad, bg, norm_w)


def _layer_norm(r, g, b):
    mu = jnp.mean(r, axis=-1, keepdims=True)
    xc = r - mu
    var = jnp.mean(xc * xc, axis=-1, keepdims=True)
    return xc * lax.rsqrt(var + NORM_EPS) * g + b


def _mix_kernel(alpha, ym_ref, yg_ref, gm_ref, gg_ref, x_ref, wbm_ref, wbg_ref, wout_ref, lg_ref, lb_ref,
                wr_ref, br_ref, x1_ref, tope_ref, gate_ref, rank_ref, cnt_ref, carry_ref):
    tm = x_ref.shape[0]
    i = pl.program_id(0)

    @pl.when(i == 0)
    def _():
        carry_ref[...] = jnp.zeros_like(carry_ref)

    pm = _dot(ym_ref[...], wbm_ref[...])
    pg = _dot(yg_ref[...], wbg_ref[...])
    z = _sigmoid(gm_ref[...]) * pm + _sigmoid(gg_ref[...]) * pg
    mix = _dot(z.astype(BF16), wout_ref[...])
    x1 = _layer_norm(alpha * x_ref[...] + mix, lg_ref[...], lb_ref[...])
    x1_ref[...] = x1

    logits = _dot_nt(wr_ref[...], x1.astype(BF16)) + br_ref[...]
    eidx = lax.broadcasted_iota(jnp.int32, (N_EXPERTS, tm), 0)
    vals, hots = [], []
    cur = logits
    for k in range(TOP_K):
        mx = jnp.max(cur, axis=0, keepdims=True)
        idx = jnp.min(jnp.where(cur == mx, eidx, N_EXPERTS), axis=0, keepdims=True)
        hot = eidx == idx
        cur = jnp.where(hot, -jnp.inf, cur)
        vals.append(mx)
        hots.append(hot)
        tope_ref[k:k + 1, :] = idx
    exps = [jnp.exp(v - vals[0]) for v in vals]
    tot = exps[0] + exps[1] + exps[2] + exps[3]
    for k in range(TOP_K):
        gate_ref[k:k + 1, :] = exps[k] / tot

    sel = jnp.zeros((N_EXPERTS, tm), F32)
    for hot in hots:
        sel = sel + jnp.where(hot, 1.0, 0.0)
    row = lax.broadcasted_iota(jnp.int32, (tm, tm), 0)
    col = lax.broadcasted_iota(jnp.int32, (tm, tm), 1)
    before = jnp.where(row < col, 1.0, 0.0).astype(BF16)
    pos = _dot(sel.astype(BF16), before) + carry_ref[:, 0:1]
    for k in range(TOP_K):
        rk = jnp.sum(jnp.where(hots[k], pos, 0.0), axis=0, keepdims=True)
        rank_ref[k:k + 1, :] = rk.astype(jnp.int32)
    carry_ref[...] = carry_ref[...] + jnp.sum(sel, axis=1, keepdims=True)
    cnt_ref[...] = carry_ref[...]


def _mix(ym, yg, proj, x2d, wbm, wbg, wout, ln_g, ln_b, wr_t, br, alpha):
    T, D = x2d.shape
    tm = min(MIX_TM, T)
    rowblk = lambda i: (i, 0)
    const = lambda i: (0, 0)
    once = dict(pipeline_mode=pl.Buffered(1))
    return pl.pallas_call(
        functools.partial(_mix_kernel, alpha),
        grid=(T // tm,),
        in_specs=[pl.BlockSpec((tm, M_VW), rowblk),
                  pl.BlockSpec((tm, G_VW), rowblk),
                  pl.BlockSpec((tm, D), lambda i: (i, 3)),
                  pl.BlockSpec((tm, D), lambda i: (i, 4)),
                  pl.BlockSpec((tm, D), rowblk),
                  pl.BlockSpec((M_VW, D), const, **once),
                  pl.BlockSpec((G_VW, D), const, **once),
                  pl.BlockSpec((D, D), const, **once),
                  pl.BlockSpec((1, D), const),
                  pl.BlockSpec((1, D), const),
                  pl.BlockSpec((N_EXPERTS, D), const),
                  pl.BlockSpec((N_EXPERTS, 1), const)],
        out_specs=[pl.BlockSpec((tm, D), rowblk),
                   pl.BlockSpec((TOP_K, tm), lambda i: (0, i)),
                   pl.BlockSpec((TOP_K, tm), lambda i: (0, i)),
                   pl.BlockSpec((TOP_K, tm), lambda i: (0, i)),
                   pl.BlockSpec((N_EXPERTS, LANES), const)],
        out_shape=[jax.ShapeDtypeStruct((T, D), F32),
                   jax.ShapeDtypeStruct((TOP_K, T), jnp.int32),
                   jax.ShapeDtypeStruct((TOP_K, T), F32),
                   jax.ShapeDtypeStruct((TOP_K, T), jnp.int32),
                   jax.ShapeDtypeStruct((N_EXPERTS, LANES), F32)],
        scratch_shapes=[pltpu.VMEM((N_EXPERTS, LANES), F32)],
        compiler_params=_params(("arbitrary",)),
    )(ym, yg, proj, proj, x2d, wbm, wbg, wout, ln_g, ln_b, wr_t, br)


def _row_copy(src, src_row, dst, dst_row, sem):
    return pltpu.make_async_copy(src.at[pl.ds(src_row, 1)], dst.at[pl.ds(dst_row, 1)], sem)


def _dispatch_kernel(n_tok, dest_ref, x1_hbm, xs_hbm, buf_ref, load_sems, row_sems):
    tm = buf_ref.shape[1]
    i = pl.program_id(0)
    n = pl.num_programs(0)

    def load(tile):
        slot = tile % 3
        return pltpu.make_async_copy(x1_hbm.at[pl.ds(pl.multiple_of(tile * tm, tm), tm)], buf_ref.at[slot],
                                     load_sems.at[slot])

    def wait_rows(tile):
        for k in range(TOP_K):
            pltpu.make_async_copy(buf_ref.at[tile % 3], xs_hbm.at[pl.ds(0, tm)], row_sems.at[tile % 2]).wait()

    @pl.when(i == 0)
    def _():
        load(0).start()

    @pl.when(i + 1 < n)
    def _():
        load(i + 1).start()

    load(i).wait()
    src = buf_ref.at[i % 3]
    base = i * tm

    def issue(t, carry):
        for k in range(TOP_K):
            _row_copy(src, t, xs_hbm, dest_ref[k * n_tok + base + t], row_sems.at[i % 2]).start()
        return carry

    lax.fori_loop(0, tm, issue, 0, unroll=ROW_DMA_UNROLL)

    @pl.when(i > 0)
    def _():
        wait_rows(i - 1)

    @pl.when(i == n - 1)
    def _():
        wait_rows(i)


def _dispatch(dest_flat, x1, n_rows):
    T, D = x1.shape
    tm = min(DISP_TM, T)
    return pl.pallas_call(
        functools.partial(_dispatch_kernel, T),
        grid_spec=pltpu.PrefetchScalarGridSpec(
            num_scalar_prefetch=1,
            grid=(T // tm,),
            in_specs=[pl.BlockSpec(memory_space=pl.ANY)],
            out_specs=pl.BlockSpec(memory_space=pl.ANY),
            scratch_shapes=[pltpu.VMEM((3, tm, D), x1.dtype),
                            pltpu.SemaphoreType.DMA((3,)),
                            pltpu.SemaphoreType.DMA((2,))]),
        out_shape=jax.ShapeDtypeStruct((n_rows, D), x1.dtype),
        compiler_params=pltpu.CompilerParams(dimension_semantics=("arbitrary",), has_side_effects=True),
    )(dest_flat, x1)


def _zero_rows_kernel(start_ref, cnt_ref, xs_in_hbm, xs_hbm, zrow_ref, sem):
    del xs_in_hbm
    zrow_ref[...] = jnp.zeros_like(zrow_ref)

    def per_range(e, carry):
        start, cnt = start_ref[e], cnt_ref[e]
        head = jnp.minimum(cnt, (-start) & (SUBLANES - 1))
        mid = start + head
        n_blk = (cnt - head) // ZERO_CHUNK
        tail = mid + n_blk * ZERO_CHUNK
        n_tail = cnt - head - n_blk * ZERO_CHUNK

        def blk_copy(r):
            dst = pl.multiple_of(mid + r * ZERO_CHUNK, SUBLANES)
            return pltpu.make_async_copy(zrow_ref, xs_hbm.at[pl.ds(dst, ZERO_CHUNK)], sem)

        def rows(first, n):
            def issue(r, c):
                _row_copy(zrow_ref, 0, xs_hbm, first + r, sem).start()
                return c

            def drain(r, c):
                _row_copy(zrow_ref, 0, xs_hbm, 0, sem).wait()
                return c

            lax.fori_loop(0, n, issue, 0)
            lax.fori_loop(0, n, drain, 0)

        def issue_blk(r, c):
            blk_copy(r).start()
            return c

        def drain_blk(r, c):
            blk_copy(0).wait()
            return c

        rows(start, head)
        lax.fori_loop(0, n_blk, issue_blk, 0)
        lax.fori_loop(0, n_blk, drain_blk, 0)
        rows(tail, n_tail)
        return carry

    lax.fori_loop(0, start_ref.shape[0], per_range, 0)


def _zero_rows(start, cnt, xs):
    return pl.pallas_call(
        _zero_rows_kernel,
        grid_spec=pltpu.PrefetchScalarGridSpec(
            num_scalar_prefetch=2,
            grid=(1,),
            in_specs=[pl.BlockSpec(memory_space=pl.ANY)],
            out_specs=pl.BlockSpec(memory_space=pl.ANY),
            scratch_shapes=[pltpu.VMEM((ZERO_CHUNK, xs.shape[1]), xs.dtype), pltpu.SemaphoreType.DMA(())]),
        out_shape=jax.ShapeDtypeStruct(xs.shape, xs.dtype),
        input_output_aliases={2: 0},
        compiler_params=pltpu.CompilerParams(dimension_semantics=("arbitrary",), has_side_effects=True),
    )(start, cnt, xs)


def _expert_kernel(te_ref, tr_ref, na_ref, xs_ref, wgu_hbm, bgu_ref, wd_hbm, bd_ref, o_ref,
                   wgu_ref, wd_ref, stage_ref, sems):
    i = pl.program_id(0)
    fh, d = wd_ref.shape
    active = i < na_ref[0]
    e = te_ref[i]
    changed = jnp.logical_or(i == 0, e != te_ref[jnp.maximum(i - 1, 0)])

    def swiglu(a, tf, f):
        glu = slice(f * tf, (f + 1) * tf)
        lin = slice(fh + f * tf, fh + (f + 1) * tf)
        g = _dot(a, wgu_ref[:, glu]) + bgu_ref[:, glu]
        l = _dot(a, wgu_ref[:, lin]) + bgu_ref[:, lin]
        a_glu = jnp.minimum(g, SWIGLU_LIMIT)
        a_lin = jnp.clip(l, -SWIGLU_LIMIT, SWIGLU_LIMIT)
        return (a_glu * _sigmoid(SWIGLU_ALPHA * a_glu) * (a_lin + 1.0)).astype(BF16)

    @pl.when(jnp.logical_and(active, changed))
    def _():
        ts = stage_ref.shape[2]
        nf = fh // ts
        n_piece = 3 * nf

        def copies(p):
            b = p % 2
            if p < 2 * nf:
                f, lin = divmod(p, 2)
                c0 = lin * fh + f * ts
                return [pltpu.make_async_copy(wgu_hbm.at[e, :, pl.ds(c0, ts)], stage_ref.at[b], sems.at[b])]
            r0 = (p - 2 * nf) * ts
            return [pltpu.make_async_copy(wd_hbm.at[e, pl.ds(r0, ts), pl.ds(j * ts, ts)],
                                          stage_ref.at[b, pl.ds(j * ts, ts), :], sems.at[b])
                    for j in range(d // ts)]

        def land(p):
            for cp in copies(p):
                cp.wait()
            b = p % 2
            if p < 2 * nf:
                f, lin = divmod(p, 2)
                c0 = lin * fh + f * ts
                wgu_ref[:, c0:c0 + ts] = stage_ref[b].astype(BF16)
            else:
                r0 = (p - 2 * nf) * ts
                for j in range(d // ts):
                    wd_ref[r0:r0 + ts, j * ts:(j + 1) * ts] = stage_ref[b, j * ts:(j + 1) * ts, :].astype(BF16)

        for p in range(2):
            for cp in copies(p):
                cp.start()
        a = xs_ref[...].astype(BF16)
        acts, acc = [], None
        for p in range(n_piece):
            land(p)
            if p + 2 < n_piece:
                for cp in copies(p + 2):
                    cp.start()
            if p < 2 * nf and p % 2 == 1:
                acts.append(swiglu(a, ts, p // 2))
            if p >= 2 * nf:
                f = p - 2 * nf
                part = _dot(acts[f], wd_ref[f * ts:(f + 1) * ts, :])
                acc = part if acc is None else acc + part
        o_ref[...] = acc + bd_ref[...]

    def ffn(a):
        tf = min(EXP_TF, fh)
        acc = None
        for f in range(fh // tf):
            part = _dot(swiglu(a, tf, f), wd_ref[f * tf:(f + 1) * tf, :])
            acc = part if acc is None else acc + part
        return acc + bd_ref[...]

    half = xs_ref.shape[0] // 2
    later = jnp.logical_and(active, jnp.logical_not(changed))
    short = tr_ref[i] <= half

    @pl.when(jnp.logical_and(later, jnp.logical_not(short)))
    def _():
        o_ref[...] = ffn(xs_ref[...].astype(BF16))

    @pl.when(jnp.logical_and(later, short))
    def _():
        o_ref[0:half, :] = ffn(xs_ref[0:half, :].astype(BF16))
        o_ref[half:, :] = jnp.zeros((half, d), F32)

    @pl.when(i >= na_ref[0])
    def _():
        o_ref[...] = jnp.zeros_like(o_ref)


def _experts(tile_e, tile_rows, n_active, xs, w_gu, b_gu, w_down, b_down):
    P, D = xs.shape
    E, _, F2 = w_gu.shape
    Fh = F2 // 2
    n_tiles = P // EXP_TM

    def tile(i, na):
        return jnp.minimum(i, na[0] - 1)

    return pl.pallas_call(
        _expert_kernel,
        grid_spec=pltpu.PrefetchScalarGridSpec(
            num_scalar_prefetch=3,
            grid=(n_tiles,),
            in_specs=[
                pl.BlockSpec((EXP_TM, D), lambda i, te, tr, na: (tile(i, na), 0)),
                pl.BlockSpec(memory_space=pl.ANY),
                pl.BlockSpec((None, 1, F2), lambda i, te, tr, na: (te[tile(i, na)], 0, 0)),
                pl.BlockSpec(memory_space=pl.ANY),
                pl.BlockSpec((None, 1, D), lambda i, te, tr, na: (te[tile(i, na)], 0, 0)),
            ],
            out_specs=pl.BlockSpec((EXP_TM, D), lambda i, te, tr, na: (i, 0)),
            scratch_shapes=[pltpu.VMEM((D, F2), BF16),
                            pltpu.VMEM((Fh, D), BF16),
                            pltpu.VMEM((2, D, EXP_WPIECE), F32),
                            pltpu.SemaphoreType.DMA((2,))]),
        out_shape=jax.ShapeDtypeStruct((P, D), F32),
        compiler_params=_params(("arbitrary",), EXP_VMEM_LIMIT),
    )(tile_e, tile_rows, n_active, xs, w_gu, b_gu, w_down, b_down)


def _combine_kernel(alpha, n_tok, dest_ref, ys_hbm, x1_ref, gate_ref, lg_ref, lb_ref, o_ref, rows_ref, sems):
    tm = x1_ref.shape[0]
    i = pl.program_id(0)
    last = pl.num_programs(0) - 1
    buf = i % 2

    def start_row(tile, b, t):
        for k in range(TOP_K):
            _row_copy(ys_hbm, dest_ref[k * n_tok + tile * tm + t], rows_ref.at[b, k], t, sems.at[b]).start()

    def wait_rows(b):
        for k in range(TOP_K):
            pltpu.make_async_copy(ys_hbm.at[pl.ds(0, tm)], rows_ref.at[b, k], sems.at[b]).wait()

    @pl.when(i == 0)
    def _():
        def issue(t, carry):
            start_row(0, 0, t)
            return carry

        lax.fori_loop(0, tm, issue, 0, unroll=ROW_DMA_UNROLL)

    wait_rows(buf)

    nxt = jnp.minimum(i + 1, last)
    grp = tm // COMB_GROUPS
    for j in range(COMB_GROUPS):
        for t in range(j * grp, (j + 1) * grp):
            start_row(nxt, 1 - buf, t)
        rs = slice(j * grp, (j + 1) * grp)
        ff = gate_ref[rs, 0:1] * rows_ref[buf, 0, rs, :]
        for k in range(1, TOP_K):
            ff = ff + gate_ref[rs, k:k + 1] * rows_ref[buf, k, rs, :]
        o_ref[rs, :] = _layer_norm(alpha * x1_ref[rs, :] + ff, lg_ref[...], lb_ref[...])

    @pl.when(i == last)
    def _():
        wait_rows(1 - buf)


def _combine(dest_flat, ys, x1, gates_col, ln_g, ln_b, alpha):
    T, D = x1.shape
    tm = min(COMB_TM, T)
    return pl.pallas_call(
        functools.partial(_combine_kernel, alpha, T),
        grid_spec=pltpu.PrefetchScalarGridSpec(
            num_scalar_prefetch=1,
            grid=(T // tm,),
            in_specs=[pl.BlockSpec(memory_space=pl.ANY),
                      pl.BlockSpec((tm, D), lambda i, d: (i, 0)),
                      pl.BlockSpec((tm, TOP_K), lambda i, d: (i, 0)),
                      pl.BlockSpec((1, D), lambda i, d: (0, 0)),
                      pl.BlockSpec((1, D), lambda i, d: (0, 0))],
            out_specs=pl.BlockSpec((tm, D), lambda i, d: (i, 0)),
            scratch_shapes=[pltpu.VMEM((2, TOP_K, tm, D), F32), pltpu.SemaphoreType.DMA((2,))]),
        out_shape=jax.ShapeDtypeStruct((T, D), F32),
        compiler_params=_params(("arbitrary",)),
    )(dest_flat, ys, x1, gates_col, ln_g, ln_b)


def _layer(x2d, B, S, alpha, w_in, b_in, conv_w, conv_b, w_gla_gate, b_gla_gate, m_norm_w, g_norm_w,
           w_branch_m, w_branch_g, w_out, ln1_g, ln1_b, w_router, b_router, w_gu, b_gu, w_down, b_down,
           ln2_g, ln2_b):
    T, D = x2d.shape
    m_end = 2 * M_QKW + 2 * M_VW
    g_beg = m_end + 2 * M_HEADS
    g_end = g_beg + 2 * G_KW + 2 * G_VW
    mg_beg = g_end + G_RANK
    main_cols = [(0, m_end), (g_beg, g_end), (mg_beg, mg_beg + 2 * D)]
    w_main = jnp.concatenate([w_in[:, a:b] for a, b in main_cols], axis=1).astype(BF16)
    b_main = jnp.concatenate([b_in[a:b] for a, b in main_cols])[None, :]
    n_small = 2 * M_HEADS + G_RANK
    w_small = jnp.concatenate([w_in[:, m_end:g_beg], w_in[:, g_end:mg_beg],
                               jnp.zeros((D, SMALL_W - n_small), F32)], axis=1).astype(BF16)
    b_small = jnp.concatenate([b_in[m_end:g_beg], b_in[g_end:mg_beg], jnp.zeros((SMALL_W - n_small,), F32)])[None, :]

    proj, small = _proj(x2d, w_main, b_main, w_small, b_small)
    small_t = small[:, :SUBLANES].T

    ym = _mlstm(proj, small, small_t, conv_w, conv_b[None, :], m_norm_w[None, :], B, S)
    wg_pad = jnp.zeros((SMALL_W, G_KW), F32).at[2 * M_HEADS:n_small].set(w_gla_gate).astype(BF16)
    yg = _gla(proj, small, wg_pad, b_gla_gate[None, :], g_norm_w[None, :], B, S)

    x1, top_e, gates, rank, cnt = _mix(
        ym, yg, proj, x2d, w_branch_m.astype(BF16), w_branch_g.astype(BF16), w_out.astype(BF16),
        ln1_g[None, :], ln1_b[None, :], w_router.T.astype(BF16), b_router[:, None], alpha)

    counts = cnt[:, 0].astype(jnp.int32)
    padded = (counts + EXP_TM - 1) // EXP_TM * EXP_TM
    pend = jnp.cumsum(padded)
    pstart = pend - padded
    n_tiles = (T * TOP_K) // EXP_TM + N_EXPERTS
    tile_start = jnp.arange(n_tiles, dtype=jnp.int32) * EXP_TM
    tile_e = jnp.sum((pend[None, :] <= tile_start[:, None]).astype(jnp.int32), axis=1)
    tile_e = jnp.minimum(tile_e, N_EXPERTS - 1)
    tile_rows = jnp.clip(counts[tile_e] - (tile_start - pstart[tile_e]), 0, EXP_TM)
    n_active = (pend[-1:] // EXP_TM).astype(jnp.int32)
    dest = rank
    for e in range(N_EXPERTS):
        dest = dest + jnp.where(top_e == e, pstart[e], 0)
    dest = dest.reshape(-1)

    xs = _dispatch(dest, x1, n_tiles * EXP_TM)
    n_rows = n_tiles * EXP_TM
    hole_start = jnp.concatenate([pstart + counts, pend[-1:]])
    hole_cnt = jnp.concatenate([padded - counts, n_rows - pend[-1:]])
    xs = _zero_rows(hole_start, hole_cnt, xs)
    ys = _experts(tile_e, tile_rows, n_active, xs, w_gu, b_gu[:, None, :], w_down, b_down[:, None, :])
    return _combine(dest, ys, x1, gates.T, ln2_g[None, :], ln2_b[None, :], alpha)


def kernel(x, w_in, b_in, conv_w, conv_b, w_gla_gate, b_gla_gate, m_norm_w, g_norm_w, w_branch_m, w_branch_g, w_out, ln1_g, ln1_b, w_router, b_router, w_gu, b_gu, w_down, b_down, ln2_g, ln2_b):
    B, S, D = x.shape
    depth = w_in.shape[0]
    alpha = (2 * depth) ** 0.25
    x2d = x.reshape(B * S, D)
    for l in range(depth):
        x2d = _layer(x2d, B, S, alpha, w_in[l], b_in[l], conv_w[l], conv_b[l], w_gla_gate[l], b_gla_gate[l],
                     m_norm_w[l], g_norm_w[l], w_branch_m[l], w_branch_g[l], w_out[l], ln1_g[l], ln1_b[l],
                     w_router[l], b_router[l], w_gu[l], b_gu[l], w_down[l], b_down[l], ln2_g[l], ln2_b[l])
    return x2d.reshape(B, S, D)
```

```python
import functools

import jax
import jax.numpy as jnp
from jax import lax
from jax.experimental import pallas as pl
from jax.experimental.pallas import tpu as pltpu

F32 = jnp.float32
BF16 = jnp.bfloat16

M_HEADS, M_QK, M_V, CONV_W = 4, 128, 256, 4
G_HEADS, G_K, G_V, G_RANK, G_TAU = 4, 128, 256, 16, 16.0
N_EXPERTS, TOP_K = 32, 4
SWIGLU_ALPHA, SWIGLU_LIMIT = 1.702, 7.0
NORM_EPS = 1e-5
M_QKW, M_VW = M_HEADS * M_QK, M_HEADS * M_V
G_KW, G_VW = G_HEADS * G_K, G_HEADS * G_V

LANES = 128
SUBLANES = 8
VMEM_LIMIT = 56 * 1024 * 1024
EXP_VMEM_LIMIT = 60 * 1024 * 1024

PROJ_TM, PROJ_TN = 1024, 1024
SMALL_W = LANES
M_CHUNK = 256
G_STEP = 256
G_CHUNK = 64
MIX_TM = 256
EXP_TM = 512
EXP_TF = 1024
EXP_WPIECE = 512
DISP_TM = 512
COMB_TM = 256
COMB_GROUPS = 8
ROW_DMA_UNROLL = 8
ZERO_CHUNK = 64


def _dot(a, b):
    return jnp.dot(a, b, preferred_element_type=F32)


def _dot_nt(a, b):
    return lax.dot_general(a, b, (((1,), (1,)), ((), ())), preferred_element_type=F32)


def _dot_tn(a, b):
    return lax.dot_general(a, b, (((0,), (0,)), ((), ())), preferred_element_type=F32)


def _split3(x):
    hi = x.astype(BF16)
    r = x - hi.astype(F32)
    mid = r.astype(BF16)
    lo = (r - mid.astype(F32)).astype(BF16)
    return hi, mid, lo


def _sigmoid(x):
    return 0.5 * jnp.tanh(0.5 * x) + 0.5


def _log_sigmoid(x):
    return jnp.minimum(x, 0.0) - jnp.log1p(jnp.exp(-jnp.abs(x)))


def _params(sem, vmem_limit=VMEM_LIMIT):
    return pltpu.CompilerParams(dimension_semantics=sem, vmem_limit_bytes=vmem_limit)


def _proj_kernel(x_ref, w_ref, b_ref, ws_ref, bs_ref, o_ref, os_ref, xb_ref):
    @pl.when(pl.program_id(1) == 0)
    def _():
        xb_ref[...] = x_ref[...].astype(BF16)
        os_ref[...] = _dot(xb_ref[...], ws_ref[...]) + bs_ref[...]

    o_ref[...] = _dot(xb_ref[...], w_ref[...]) + b_ref[...]


def _proj(x, w_bf, b, ws_bf, bs):
    T, K = x.shape
    N = w_bf.shape[1]
    tm = min(PROJ_TM, T)
    tn = PROJ_TN
    return pl.pallas_call(
        _proj_kernel,
        grid=(T // tm, N // tn),
        in_specs=[pl.BlockSpec((tm, K), lambda i, j: (i, 0)),
                  pl.BlockSpec((K, tn), lambda i, j: (0, j)),
                  pl.BlockSpec((1, tn), lambda i, j: (0, j)),
                  pl.BlockSpec((K, SMALL_W), lambda i, j: (0, 0)),
                  pl.BlockSpec((1, SMALL_W), lambda i, j: (0, 0))],
        out_specs=[pl.BlockSpec((tm, tn), lambda i, j: (i, j)),
                   pl.BlockSpec((tm, SMALL_W), lambda i, j: (i, 0))],
        out_shape=[jax.ShapeDtypeStruct((T, N), F32), jax.ShapeDtypeStruct((T, SMALL_W), F32)],
        scratch_shapes=[pltpu.VMEM((tm, K), BF16)],
        compiler_params=_params(("parallel", "arbitrary")),
    )(x, w_bf, b, ws_bf, bs)


def _mlstm_kernel(qk_ref, v_ref, og_ref, sm_ref, smt_ref, cw_ref, cb_ref, nw_ref, o_ref,
                  ext_ref, c_ref, n_ref, m_ref):
    L = M_CHUNK
    c = pl.program_id(1)

    @pl.when(c == 0)
    def _():
        ext_ref[0:SUBLANES, :] = jnp.zeros((SUBLANES, 2 * M_QKW), F32)
        c_ref[...] = jnp.zeros_like(c_ref)
        n_ref[...] = jnp.zeros_like(n_ref)
        m_ref[...] = jnp.zeros_like(m_ref)

    ext_ref[SUBLANES:, :] = qk_ref[...]
    acc = cb_ref[...] + cw_ref[CONV_W - 1:CONV_W, :] * ext_ref[SUBLANES:, :]
    for j in range(CONV_W - 1):
        off = SUBLANES - (CONV_W - 1) + j
        acc = acc + cw_ref[j:j + 1, :] * ext_ref[off:off + L, :]
    ext_ref[0:SUBLANES, :] = ext_ref[L:L + SUBLANES, :]
    qk = acc * _sigmoid(acc)

    row = lax.broadcasted_iota(jnp.int32, (L, L), 0)
    col = lax.broadcasted_iota(jnp.int32, (L, L), 1)
    causal = row >= col
    tri = jnp.where(causal, 1.0, 0.0).astype(BF16)
    triu = jnp.where(col >= row, 1.0, 0.0).astype(BF16)

    sm = sm_ref[...]
    smt = smt_ref[...]
    h1, h2, h3 = _split3(_log_sigmoid(sm))
    bcol_all = _dot(tri, h1) + _dot(tri, h2) + _dot(tri, h3)
    r1, r2, r3 = _split3(_log_sigmoid(smt))
    brow_all = _dot(r1, triu) + _dot(r2, triu) + _dot(r3, triu)

    for h in range(M_HEADS):
        q = qk[:, h * M_QK:(h + 1) * M_QK]
        k = qk[:, M_QKW + h * M_QK:M_QKW + (h + 1) * M_QK] * (M_QK ** -0.5)
        v = v_ref[:, h * M_V:(h + 1) * M_V]
        qb, kb, vb = q.astype(BF16), k.astype(BF16), v.astype(BF16)
        li_col = sm[:, h:h + 1]
        b_col = bcol_all[:, M_HEADS + h:M_HEADS + h + 1]
        b_end = b_col[L - 1:L, :]
        li_row = smt[h:h + 1, :]
        b_row = brow_all[M_HEADS + h:M_HEADS + h + 1, :]
        m_prev = m_ref[h:h + 1, 0:1]
        n_prev = n_ref[h:h + 1, :]
        c_prev = c_ref[h]

        d_log = jnp.where(causal, b_col + (li_row - b_row), -jnp.inf)
        m_inter = b_col + m_prev
        m_t = jnp.maximum(m_inter, jnp.max(d_log, axis=-1, keepdims=True))
        dec = jnp.exp(m_inter - m_t)
        s = _dot_nt(qb, kb) * jnp.exp(d_log - m_t)
        num = _dot(s.astype(BF16), vb) + dec * _dot(qb, c_prev.astype(BF16))
        den = jnp.sum(s, axis=-1, keepdims=True) + dec * jnp.sum(q * n_prev, axis=-1, keepdims=True)
        hh = num / jnp.maximum(jnp.abs(den), jnp.exp(-m_t))

        w_end = b_end - b_col + li_col
        g_end = jnp.max(w_end, axis=0, keepdims=True)
        ke = k * jnp.exp(w_end - g_end)
        d_c = _dot_tn(ke.astype(BF16), vb)
        d_n = jnp.sum(ke, axis=0, keepdims=True)
        m_new = jnp.maximum(b_end + m_prev, g_end)
        a = jnp.exp(b_end + m_prev - m_new)
        cc = jnp.exp(g_end - m_new)
        c_ref[h] = a * c_prev + cc * d_c
        n_ref[h:h + 1, :] = a * n_prev + cc * d_n
        m_ref[h:h + 1, :] = jnp.broadcast_to(m_new, (1, LANES))

        mu = jnp.mean(hh, axis=-1, keepdims=True)
        xc = hh - mu
        var = jnp.mean(xc * xc, axis=-1, keepdims=True)
        sl = slice(h * M_V, (h + 1) * M_V)
        y = xc * lax.rsqrt(var + NORM_EPS) * nw_ref[:, sl] * _sigmoid(og_ref[:, sl])
        o_ref[:, sl] = y.astype(o_ref.dtype)


def _mlstm(proj, small, small_t, conv_w, conv_b, norm_w, B, S):
    L = M_CHUNK
    NC = S // L
    T = B * S
    rowblk = lambda b, c: (b * NC + c, 0)
    const = lambda b, c: (0, 0)
    return pl.pallas_call(
        _mlstm_kernel,
        grid=(B, NC),
        in_specs=[pl.BlockSpec((L, 2 * M_QKW), rowblk),
                  pl.BlockSpec((L, M_VW), lambda b, c: (b * NC + c, 1)),
                  pl.BlockSpec((L, M_VW), lambda b, c: (b * NC + c, 2)),
                  pl.BlockSpec((L, SMALL_W), rowblk),
                  pl.BlockSpec((SUBLANES, L), lambda b, c: (0, b * NC + c)),
                  pl.BlockSpec((CONV_W, 2 * M_QKW), const),
                  pl.BlockSpec((1, 2 * M_QKW), const),
                  pl.BlockSpec((1, M_VW), const)],
        out_specs=pl.BlockSpec((L, M_VW), rowblk),
        out_shape=jax.ShapeDtypeStruct((T, M_VW), BF16),
        scratch_shapes=[pltpu.VMEM((L + SUBLANES, 2 * M_QKW), F32),
                        pltpu.VMEM((M_HEADS, M_QK, M_V), F32),
                        pltpu.VMEM((SUBLANES, M_QK), F32),
                        pltpu.VMEM((SUBLANES, LANES), F32)],
        compiler_params=_params(("parallel", "arbitrary")),
    )(proj, proj, proj, small, small_t, conv_w, conv_b, norm_w)


def _gla_kernel(qk_ref, v_ref, gg_ref, sm_ref, wg_ref, bg_ref, nw_ref, o_ref, st_ref):
    LC = G_CHUNK
    c = pl.program_id(1)

    @pl.when(c == 0)
    def _():
        st_ref[...] = jnp.zeros_like(st_ref)

    pre = _dot(sm_ref[...].astype(BF16), wg_ref[...]) + bg_ref[...]
    log_a = _log_sigmoid(pre) / G_TAU

    row = lax.broadcasted_iota(jnp.int32, (LC, LC), 0)
    col = lax.broadcasted_iota(jnp.int32, (LC, LC), 1)
    causal = row >= col
    tri = jnp.where(causal, 1.0, 0.0).astype(BF16)

    for j in range(G_STEP // LC):
        rs = slice(j * LC, (j + 1) * LC)
        a1, a2, a3 = _split3(log_a[rs, :])
        cum = _dot(tri, a1) + _dot(tri, a2) + _dot(tri, a3)
        cum_end = cum[LC - 1:LC, :]
        e_q = jnp.exp(cum)
        e_k = jnp.exp(-cum)
        e_end = jnp.exp(cum_end - cum)
        e_dec = jnp.exp(cum_end)
        for h in range(G_HEADS):
            ks = slice(h * G_K, (h + 1) * G_K)
            vs = slice(h * G_V, (h + 1) * G_V)
            q = qk_ref[rs, ks] * (G_K ** -0.5)
            k = qk_ref[rs, G_KW + h * G_K:G_KW + (h + 1) * G_K]
            vb = v_ref[rs, vs].astype(BF16)
            q_in = (q * e_q[:, ks]).astype(BF16)
            k_in = (k * e_k[:, ks]).astype(BF16)
            k_end = (k * e_end[:, ks]).astype(BF16)
            st = st_ref[h]
            s = jnp.where(causal, _dot_nt(q_in, k_in), 0.0)
            o = _dot(s.astype(BF16), vb) + _dot_nt(q_in, st.astype(BF16))
            st_ref[h] = st * e_dec[:, ks] + _dot_tn(vb, k_end)
            ms = jnp.mean(o * o, axis=-1, keepdims=True)
            g = gg_ref[rs, vs]
            y = o * lax.rsqrt(ms + NORM_EPS) * nw_ref[:, vs] * (g * _sigmoid(g))
            o_ref[rs, vs] = y.astype(o_ref.dtype)


def _gla(proj, small, wg_pad, bg, norm_w, B, S):
    L = G_STEP
    NC = S // L
    T = B * S
    rowblk = lambda b, c: (b * NC + c, 0)
    const = lambda b, c: (0, 0)
    return pl.pallas_call(
        _gla_kernel,
        grid=(B, NC),
        in_specs=[pl.BlockSpec((L, 2 * G_KW), lambda b, c: (b * NC + c, 3)),
                  pl.BlockSpec((L, G_VW), lambda b, c: (b * NC + c, 4)),
                  pl.BlockSpec((L, G_VW), lambda b, c: (b * NC + c, 5)),
                  pl.BlockSpec((L, SMALL_W), rowblk),
                  pl.BlockSpec((SMALL_W, G_KW), const),
                  pl.BlockSpec((1, G_KW), const),
                  pl.BlockSpec((1, G_VW), const)],
        out_specs=pl.BlockSpec((L, G_VW), rowblk),
        out_shape=jax.ShapeDtypeStruct((T, G_VW), BF16),
        scratch_shapes=[pltpu.VMEM((G_HEADS, G_V, G_K), F32)],
        compiler_params=_params(("parallel", "arbitrary")),
    )(proj, proj, proj, small, wg_pad, bg, norm_w)


def _layer_norm(r, g, b):
    mu = jnp.mean(r, axis=-1, keepdims=True)
    xc = r - mu
    var = jnp.mean(xc * xc, axis=-1, keepdims=True)
    return xc * lax.rsqrt(var + NORM_EPS) * g + b


def _mix_kernel(alpha, ym_ref, yg_ref, gm_ref, gg_ref, x_ref, wbm_ref, wbg_ref, wout_ref, lg_ref, lb_ref,
                wr_ref, br_ref, x1_ref, tope_ref, gate_ref, rank_ref, cnt_ref, carry_ref):
    tm = x_ref.shape[0]
    i = pl.program_id(0)

    @pl.when(i == 0)
    def _():
        carry_ref[...] = jnp.zeros_like(carry_ref)

    pm = _dot(ym_ref[...], wbm_ref[...])
    pg = _dot(yg_ref[...], wbg_ref[...])
    z = _sigmoid(gm_ref[...]) * pm + _sigmoid(gg_ref[...]) * pg
    mix = _dot(z.astype(BF16), wout_ref[...])
    x1 = _layer_norm(alpha * x_ref[...] + mix, lg_ref[...], lb_ref[...])
    x1_ref[...] = x1

    logits = _dot_nt(wr_ref[...], x1.astype(BF16)) + br_ref[...]
    eidx = lax.broadcasted_iota(jnp.int32, (N_EXPERTS, tm), 0)
    vals, hots = [], []
    cur = logits
    for k in range(TOP_K):
        mx = jnp.max(cur, axis=0, keepdims=True)
        idx = jnp.min(jnp.where(cur == mx, eidx, N_EXPERTS), axis=0, keepdims=True)
        hot = eidx == idx
        cur = jnp.where(hot, -jnp.inf, cur)
        vals.append(mx)
        hots.append(hot)
        tope_ref[k:k + 1, :] = idx
    exps = [jnp.exp(v - vals[0]) for v in vals]
    tot = exps[0] + exps[1] + exps[2] + exps[3]
    for k in range(TOP_K):
        gate_ref[k:k + 1, :] = exps[k] / tot

    sel = jnp.zeros((N_EXPERTS, tm), F32)
    for hot in hots:
        sel = sel + jnp.where(hot, 1.0, 0.0)
    row = lax.broadcasted_iota(jnp.int32, (tm, tm), 0)
    col = lax.broadcasted_iota(jnp.int32, (tm, tm), 1)
    before = jnp.where(row < col, 1.0, 0.0).astype(BF16)
    pos = _dot(sel.astype(BF16), before) + carry_ref[:, 0:1]
    for k in range(TOP_K):
        rk = jnp.sum(jnp.where(hots[k], pos, 0.0), axis=0, keepdims=True)
        rank_ref[k:k + 1, :] = rk.astype(jnp.int32)
    carry_ref[...] = carry_ref[...] + jnp.sum(sel, axis=1, keepdims=True)
    cnt_ref[...] = carry_ref[...]


def _mix(ym, yg, proj, x2d, wbm, wbg, wout, ln_g, ln_b, wr_t, br, alpha):
    T, D = x2d.shape
    tm = min(MIX_TM, T)
    rowblk = lambda i: (i, 0)
    const = lambda i: (0, 0)
    once = dict(pipeline_mode=pl.Buffered(1))
    return pl.pallas_call(
        functools.partial(_mix_kernel, alpha),
        grid=(T // tm,),
        in_specs=[pl.BlockSpec((tm, M_VW), rowblk),
                  pl.BlockSpec((tm, G_VW), rowblk),
                  pl.BlockSpec((tm, D), lambda i: (i, 3)),
                  pl.BlockSpec((tm, D), lambda i: (i, 4)),
                  pl.BlockSpec((tm, D), rowblk),
                  pl.BlockSpec((M_VW, D), const, **once),
                  pl.BlockSpec((G_VW, D), const, **once),
                  pl.BlockSpec((D, D), const, **once),
                  pl.BlockSpec((1, D), const),
                  pl.BlockSpec((1, D), const),
                  pl.BlockSpec((N_EXPERTS, D), const),
                  pl.BlockSpec((N_EXPERTS, 1), const)],
        out_specs=[pl.BlockSpec((tm, D), rowblk),
                   pl.BlockSpec((TOP_K, tm), lambda i: (0, i)),
                   pl.BlockSpec((TOP_K, tm), lambda i: (0, i)),
                   pl.BlockSpec((TOP_K, tm), lambda i: (0, i)),
                   pl.BlockSpec((N_EXPERTS, LANES), const)],
        out_shape=[jax.ShapeDtypeStruct((T, D), F32),
                   jax.ShapeDtypeStruct((TOP_K, T), jnp.int32),
                   jax.ShapeDtypeStruct((TOP_K, T), F32),
                   jax.ShapeDtypeStruct((TOP_K, T), jnp.int32),
                   jax.ShapeDtypeStruct((N_EXPERTS, LANES), F32)],
        scratch_shapes=[pltpu.VMEM((N_EXPERTS, LANES), F32)],
        compiler_params=_params(("arbitrary",)),
    )(ym, yg, proj, proj, x2d, wbm, wbg, wout, ln_g, ln_b, wr_t, br)


def _row_copy(src, src_row, dst, dst_row, sem):
    return pltpu.make_async_copy(src.at[pl.ds(src_row, 1)], dst.at[pl.ds(dst_row, 1)], sem)


def _dispatch_kernel(n_tok, dest_ref, x1_hbm, xs_hbm, buf_ref, load_sems, row_sems):
    tm = buf_ref.shape[1]
    i = pl.program_id(0)
    n = pl.num_programs(0)

    def load(tile):
        slot = tile % 3
        return pltpu.make_async_copy(x1_hbm.at[pl.ds(pl.multiple_of(tile * tm, tm), tm)], buf_ref.at[slot],
                                     load_sems.at[slot])

    def wait_rows(tile):
        for k in range(TOP_K):
            pltpu.make_async_copy(buf_ref.at[tile % 3], xs_hbm.at[pl.ds(0, tm)], row_sems.at[tile % 2]).wait()

    @pl.when(i == 0)
    def _():
        load(0).start()

    @pl.when(i + 1 < n)
    def _():
        load(i + 1).start()

    load(i).wait()
    src = buf_ref.at[i % 3]
    base = i * tm

    def issue(t, carry):
        for k in range(TOP_K):
            _row_copy(src, t, xs_hbm, dest_ref[k * n_tok + base + t], row_sems.at[i % 2]).start()
        return carry

    lax.fori_loop(0, tm, issue, 0, unroll=ROW_DMA_UNROLL)

    @pl.when(i > 0)
    def _():
        wait_rows(i - 1)

    @pl.when(i == n - 1)
    def _():
        wait_rows(i)


def _dispatch(dest_flat, x1, n_rows):
    T, D = x1.shape
    tm = min(DISP_TM, T)
    return pl.pallas_call(
        functools.partial(_dispatch_kernel, T),
        grid_spec=pltpu.PrefetchScalarGridSpec(
            num_scalar_prefetch=1,
            grid=(T // tm,),
            in_specs=[pl.BlockSpec(memory_space=pl.ANY)],
            out_specs=pl.BlockSpec(memory_space=pl.ANY),
            scratch_shapes=[pltpu.VMEM((3, tm, D), x1.dtype),
                            pltpu.SemaphoreType.DMA((3,)),
                            pltpu.SemaphoreType.DMA((2,))]),
        out_shape=jax.ShapeDtypeStruct((n_rows, D), x1.dtype),
        compiler_params=pltpu.CompilerParams(dimension_semantics=("arbitrary",), has_side_effects=True),
    )(dest_flat, x1)


def _zero_rows_kernel(start_ref, cnt_ref, xs_in_hbm, xs_hbm, zrow_ref, sem):
    del xs_in_hbm
    zrow_ref[...] = jnp.zeros_like(zrow_ref)

    def per_range(e, carry):
        start, cnt = start_ref[e], cnt_ref[e]
        head = jnp.minimum(cnt, (-start) & (SUBLANES - 1))
        mid = start + head
        n_big = (cnt - head) // ZERO_CHUNK
        mid2 = mid + n_big * ZERO_CHUNK
        n_small = (cnt - head - n_big * ZERO_CHUNK) // SUBLANES
        tail = mid2 + n_small * SUBLANES
        n_tail = cnt - head - n_big * ZERO_CHUNK - n_small * SUBLANES

        def blocks(first, n, size):
            def copy(r):
                dst = pl.multiple_of(first + r * size, SUBLANES)
                return pltpu.make_async_copy(zrow_ref.at[0:size], xs_hbm.at[pl.ds(dst, size)], sem)

            def issue(r, c):
                copy(r).start()
                return c

            def drain(r, c):
                copy(0).wait()
                return c

            lax.fori_loop(0, n, issue, 0)
            lax.fori_loop(0, n, drain, 0)

        def rows(first, n):
            def issue(r, c):
                _row_copy(zrow_ref, 0, xs_hbm, first + r, sem).start()
                return c

            def drain(r, c):
                _row_copy(zrow_ref, 0, xs_hbm, 0, sem).wait()
                return c

            lax.fori_loop(0, n, issue, 0)
            lax.fori_loop(0, n, drain, 0)

        rows(start, head)
        blocks(mid, n_big, ZERO_CHUNK)
        blocks(mid2, n_small, SUBLANES)
        rows(tail, n_tail)
        return carry

    lax.fori_loop(0, start_ref.shape[0], per_range, 0)


def _zero_rows(start, cnt, xs):
    return pl.pallas_call(
        _zero_rows_kernel,
        grid_spec=pltpu.PrefetchScalarGridSpec(
            num_scalar_prefetch=2,
            grid=(1,),
            in_specs=[pl.BlockSpec(memory_space=pl.ANY)],
            out_specs=pl.BlockSpec(memory_space=pl.ANY),
            scratch_shapes=[pltpu.VMEM((ZERO_CHUNK, xs.shape[1]), xs.dtype), pltpu.SemaphoreType.DMA(())]),
        out_shape=jax.ShapeDtypeStruct(xs.shape, xs.dtype),
        input_output_aliases={2: 0},
        compiler_params=pltpu.CompilerParams(dimension_semantics=("arbitrary",), has_side_effects=True),
    )(start, cnt, xs)


def _expert_kernel(te_ref, tr_ref, na_ref, xs_ref, wgu_hbm, bgu_ref, wd_hbm, bd_ref, o_ref,
                   wgu_ref, wd_ref, stage_ref, sems):
    i = pl.program_id(0)
    fh, d = wd_ref.shape
    active = i < na_ref[0]
    e = te_ref[i]
    changed = jnp.logical_or(i == 0, e != te_ref[jnp.maximum(i - 1, 0)])

    def swiglu(a, tf, f):
        glu = slice(f * tf, (f + 1) * tf)
        lin = slice(fh + f * tf, fh + (f + 1) * tf)
        g = _dot(a, wgu_ref[:, glu]) + bgu_ref[:, glu]
        l = _dot(a, wgu_ref[:, lin]) + bgu_ref[:, lin]
        a_glu = jnp.minimum(g, SWIGLU_LIMIT)
        a_lin = jnp.clip(l, -SWIGLU_LIMIT, SWIGLU_LIMIT)
        return (a_glu * _sigmoid(SWIGLU_ALPHA * a_glu) * (a_lin + 1.0)).astype(BF16)

    @pl.when(jnp.logical_and(active, changed))
    def _():
        ts = stage_ref.shape[2]
        nf = fh // ts
        n_piece = 3 * nf

        def copies(p):
            b = p % 2
            if p < 2 * nf:
                f, lin = divmod(p, 2)
                c0 = lin * fh + f * ts
                return [pltpu.make_async_copy(wgu_hbm.at[e, :, pl.ds(c0, ts)], stage_ref.at[b], sems.at[b])]
            r0 = (p - 2 * nf) * ts
            return [pltpu.make_async_copy(wd_hbm.at[e, pl.ds(r0, ts), pl.ds(j * ts, ts)],
                                          stage_ref.at[b, pl.ds(j * ts, ts), :], sems.at[b])
                    for j in range(d // ts)]

        def land(p):
            for cp in copies(p):
                cp.wait()
            b = p % 2
            if p < 2 * nf:
                f, lin = divmod(p, 2)
                c0 = lin * fh + f * ts
                wgu_ref[:, c0:c0 + ts] = stage_ref[b].astype(BF16)
            else:
                r0 = (p - 2 * nf) * ts
                for j in range(d // ts):
                    wd_ref[r0:r0 + ts, j * ts:(j + 1) * ts] = stage_ref[b, j * ts:(j + 1) * ts, :].astype(BF16)

        for p in range(2):
            for cp in copies(p):
                cp.start()
        a = xs_ref[...].astype(BF16)
        acts, acc = [], None
        for p in range(n_piece):
            land(p)
            if p + 2 < n_piece:
                for cp in copies(p + 2):
                    cp.start()
            if p < 2 * nf and p % 2 == 1:
                acts.append(swiglu(a, ts, p // 2))
            if p >= 2 * nf:
                f = p - 2 * nf
                part = _dot(acts[f], wd_ref[f * ts:(f + 1) * ts, :])
                acc = part if acc is None else acc + part
        o_ref[...] = acc + bd_ref[...]

    def ffn(a):
        tf = min(EXP_TF, fh)
        acc = None
        for f in range(fh // tf):
            part = _dot(swiglu(a, tf, f), wd_ref[f * tf:(f + 1) * tf, :])
            acc = part if acc is None else acc + part
        return acc + bd_ref[...]

    half = xs_ref.shape[0] // 2
    later = jnp.logical_and(active, jnp.logical_not(changed))
    short = tr_ref[i] <= half

    @pl.when(jnp.logical_and(later, jnp.logical_not(short)))
    def _():
        o_ref[...] = ffn(xs_ref[...].astype(BF16))

    @pl.when(jnp.logical_and(later, short))
    def _():
        o_ref[0:half, :] = ffn(xs_ref[0:half, :].astype(BF16))
        o_ref[half:, :] = jnp.zeros((half, d), F32)

    @pl.when(i >= na_ref[0])
    def _():
        o_ref[...] = jnp.zeros_like(o_ref)


def _experts(tile_e, tile_rows, n_active, xs, w_gu, b_gu, w_down, b_down):
    P, D = xs.shape
    E, _, F2 = w_gu.shape
    Fh = F2 // 2
    n_tiles = P // EXP_TM

    def tile(i, na):
        return jnp.minimum(i, na[0] - 1)

    return pl.pallas_call(
        _expert_kernel,
        grid_spec=pltpu.PrefetchScalarGridSpec(
            num_scalar_prefetch=3,
            grid=(n_tiles,),
            in_specs=[
                pl.BlockSpec((EXP_TM, D), lambda i, te, tr, na: (tile(i, na), 0)),
                pl.BlockSpec(memory_space=pl.ANY),
                pl.BlockSpec((None, 1, F2), lambda i, te, tr, na: (te[tile(i, na)], 0, 0)),
                pl.BlockSpec(memory_space=pl.ANY),
                pl.BlockSpec((None, 1, D), lambda i, te, tr, na: (te[tile(i, na)], 0, 0)),
            ],
            out_specs=pl.BlockSpec((EXP_TM, D), lambda i, te, tr, na: (i, 0)),
            scratch_shapes=[pltpu.VMEM((D, F2), BF16),
                            pltpu.VMEM((Fh, D), BF16),
                            pltpu.VMEM((2, D, EXP_WPIECE), F32),
                            pltpu.SemaphoreType.DMA((2,))]),
        out_shape=jax.ShapeDtypeStruct((P, D), F32),
        compiler_params=_params(("arbitrary",), EXP_VMEM_LIMIT),
    )(tile_e, tile_rows, n_active, xs, w_gu, b_gu, w_down, b_down)


def _combine_kernel(alpha, n_tok, dest_ref, ys_hbm, x1_ref, gate_ref, lg_ref, lb_ref, o_ref, rows_ref, sems):
    tm = x1_ref.shape[0]
    i = pl.program_id(0)
    last = pl.num_programs(0) - 1
    buf = i % 2

    def start_row(tile, b, t):
        for k in range(TOP_K):
            _row_copy(ys_hbm, dest_ref[k * n_tok + tile * tm + t], rows_ref.at[b, k], t, sems.at[b]).start()

    def wait_rows(b):
        for k in range(TOP_K):
            pltpu.make_async_copy(ys_hbm.at[pl.ds(0, tm)], rows_ref.at[b, k], sems.at[b]).wait()

    @pl.when(i == 0)
    def _():
        def issue(t, carry):
            start_row(0, 0, t)
            return carry

        lax.fori_loop(0, tm, issue, 0, unroll=ROW_DMA_UNROLL)

    wait_rows(buf)

    nxt = jnp.minimum(i + 1, last)
    grp = tm // COMB_GROUPS
    for j in range(COMB_GROUPS):
        for t in range(j * grp, (j + 1) * grp):
            start_row(nxt, 1 - buf, t)
        rs = slice(j * grp, (j + 1) * grp)
        ff = gate_ref[rs, 0:1] * rows_ref[buf, 0, rs, :]
        for k in range(1, TOP_K):
            ff = ff + gate_ref[rs, k:k + 1] * rows_ref[buf, k, rs, :]
        o_ref[rs, :] = _layer_norm(alpha * x1_ref[rs, :] + ff, lg_ref[...], lb_ref[...])

    @pl.when(i == last)
    def _():
        wait_rows(1 - buf)


def _combine(dest_flat, ys, x1, gates_col, ln_g, ln_b, alpha):
    T, D = x1.shape
    tm = min(COMB_TM, T)
    return pl.pallas_call(
        functools.partial(_combine_kernel, alpha, T),
        grid_spec=pltpu.PrefetchScalarGridSpec(
            num_scalar_prefetch=1,
            grid=(T // tm,),
            in_specs=[pl.BlockSpec(memory_space=pl.ANY),
                      pl.BlockSpec((tm, D), lambda i, d: (i, 0)),
                      pl.BlockSpec((tm, TOP_K), lambda i, d: (i, 0)),
                      pl.BlockSpec((1, D), lambda i, d: (0, 0)),
                      pl.BlockSpec((1, D), lambda i, d: (0, 0))],
            out_specs=pl.BlockSpec((tm, D), lambda i, d: (i, 0)),
            scratch_shapes=[pltpu.VMEM((2, TOP_K, tm, D), F32), pltpu.SemaphoreType.DMA((2,))]),
        out_shape=jax.ShapeDtypeStruct((T, D), F32),
        compiler_params=_params(("arbitrary",)),
    )(dest_flat, ys, x1, gates_col, ln_g, ln_b)


def _layer(x2d, B, S, alpha, w_in, b_in, conv_w, conv_b, w_gla_gate, b_gla_gate, m_norm_w, g_norm_w,
           w_branch_m, w_branch_g, w_out, ln1_g, ln1_b, w_router, b_router, w_gu, b_gu, w_down, b_down,
           ln2_g, ln2_b):
    T, D = x2d.shape
    m_end = 2 * M_QKW + 2 * M_VW
    g_beg = m_end + 2 * M_HEADS
    g_end = g_beg + 2 * G_KW + 2 * G_VW
    mg_beg = g_end + G_RANK
    main_cols = [(0, m_end), (g_beg, g_end), (mg_beg, mg_beg + 2 * D)]
    w_main = jnp.concatenate([w_in[:, a:b] for a, b in main_cols], axis=1).astype(BF16)
    b_main = jnp.concatenate([b_in[a:b] for a, b in main_cols])[None, :]
    n_small = 2 * M_HEADS + G_RANK
    w_small = jnp.concatenate([w_in[:, m_end:g_beg], w_in[:, g_end:mg_beg],
                               jnp.zeros((D, SMALL_W - n_small), F32)], axis=1).astype(BF16)
    b_small = jnp.concatenate([b_in[m_end:g_beg], b_in[g_end:mg_beg], jnp.zeros((SMALL_W - n_small,), F32)])[None, :]

    proj, small = _proj(x2d, w_main, b_main, w_small, b_small)
    small_t = small[:, :SUBLANES].T

    ym = _mlstm(proj, small, small_t, conv_w, conv_b[None, :], m_norm_w[None, :], B, S)
    wg_pad = jnp.zeros((SMALL_W, G_KW), F32).at[2 * M_HEADS:n_small].set(w_gla_gate).astype(BF16)
    yg = _gla(proj, small, wg_pad, b_gla_gate[None, :], g_norm_w[None, :], B, S)

    x1, top_e, gates, rank, cnt = _mix(
        ym, yg, proj, x2d, w_branch_m.astype(BF16), w_branch_g.astype(BF16), w_out.astype(BF16),
        ln1_g[None, :], ln1_b[None, :], w_router.T.astype(BF16), b_router[:, None], alpha)

    counts = cnt[:, 0].astype(jnp.int32)
    padded = (counts + EXP_TM - 1) // EXP_TM * EXP_TM
    pend = jnp.cumsum(padded)
    pstart = pend - padded
    n_tiles = (T * TOP_K) // EXP_TM + N_EXPERTS
    tile_start = jnp.arange(n_tiles, dtype=jnp.int32) * EXP_TM
    tile_e = jnp.sum((pend[None, :] <= tile_start[:, None]).astype(jnp.int32), axis=1)
    tile_e = jnp.minimum(tile_e, N_EXPERTS - 1)
    tile_rows = jnp.clip(counts[tile_e] - (tile_start - pstart[tile_e]), 0, EXP_TM)
    n_active = (pend[-1:] // EXP_TM).astype(jnp.int32)
    dest = rank
    for e in range(N_EXPERTS):
        dest = dest + jnp.where(top_e == e, pstart[e], 0)
    dest = dest.reshape(-1)

    xs = _dispatch(dest, x1, n_tiles * EXP_TM)
    n_rows = n_tiles * EXP_TM
    hole_start = jnp.concatenate([pstart + counts, pend[-1:]])
    hole_cnt = jnp.concatenate([padded - counts, n_rows - pend[-1:]])
    xs = _zero_rows(hole_start, hole_cnt, xs)
    ys = _experts(tile_e, tile_rows, n_active, xs, w_gu, b_gu[:, None, :], w_down, b_down[:, None, :])
    return _combine(dest, ys, x1, gates.T, ln2_g[None, :], ln2_b[None, :], alpha)


def kernel(x, w_in, b_in, conv_w, conv_b, w_gla_gate, b_gla_gate, m_norm_w, g_norm_w, w_branch_m, w_branch_g, w_out, ln1_g, ln1_b, w_router, b_router, w_gu, b_gu, w_down, b_down, ln2_g, ln2_b):
    B, S, D = x.shape
    depth = w_in.shape[0]
    alpha = (2 * depth) ** 0.25
    x2d = x.reshape(B * S, D)
    for l in range(depth):
        x2d = _layer(x2d, B, S, alpha, w_in[l], b_in[l], conv_w[l], conv_b[l], w_gla_gate[l], b_gla_gate[l],
                     m_norm_w[l], g_norm_w[l], w_branch_m[l], w_branch_g[l], w_out[l], ln1_g[l], ln1_b[l],
                     w_router[l], b_router[l], w_gu[l], b_gu[l], w_down[l], b_down[l], ln2_g[l], ln2_b[l])
    return x2d.reshape(B, S, D)
```

```python
import functools

import jax
import jax.numpy as jnp
from jax import lax
from jax.experimental import pallas as pl
from jax.experimental.pallas import tpu as pltpu

F32 = jnp.float32
BF16 = jnp.bfloat16

M_HEADS, M_QK, M_V, CONV_W = 4, 128, 256, 4
G_HEADS, G_K, G_V, G_RANK, G_TAU = 4, 128, 256, 16, 16.0
N_EXPERTS, TOP_K = 32, 4
SWIGLU_ALPHA, SWIGLU_LIMIT = 1.702, 7.0
NORM_EPS = 1e-5
M_QKW, M_VW = M_HEADS * M_QK, M_HEADS * M_V
G_KW, G_VW = G_HEADS * G_K, G_HEADS * G_V

LANES = 128
SUBLANES = 8
VMEM_LIMIT = 56 * 1024 * 1024
EXP_VMEM_LIMIT = 60 * 1024 * 1024

PROJ_TM, PROJ_TN = 1024, 1024
SMALL_W = LANES
M_CHUNK = 256
G_STEP = 512
G_CHUNK = 64
MIX_TM = 256
EXP_TM = 512
EXP_TF = 1024
EXP_WPIECE = 512
DISP_TM = 512
COMB_TM = 256
COMB_GROUPS = 4
ROW_DMA_UNROLL = 8
ZERO_CHUNK = 64


def _dot(a, b):
    return jnp.dot(a, b, preferred_element_type=F32)


def _dot_nt(a, b):
    return lax.dot_general(a, b, (((1,), (1,)), ((), ())), preferred_element_type=F32)


def _dot_tn(a, b):
    return lax.dot_general(a, b, (((0,), (0,)), ((), ())), preferred_element_type=F32)


def _split3(x):
    hi = x.astype(BF16)
    r = x - hi.astype(F32)
    mid = r.astype(BF16)
    lo = (r - mid.astype(F32)).astype(BF16)
    return hi, mid, lo


def _sigmoid(x):
    return 0.5 * jnp.tanh(0.5 * x) + 0.5


def _log_sigmoid(x):
    return jnp.minimum(x, 0.0) - jnp.log1p(jnp.exp(-jnp.abs(x)))


def _params(sem, vmem_limit=VMEM_LIMIT):
    return pltpu.CompilerParams(dimension_semantics=sem, vmem_limit_bytes=vmem_limit)


def _proj_kernel(x_ref, w_ref, b_ref, ws_ref, bs_ref, o_ref, os_ref, xb_ref):
    @pl.when(pl.program_id(1) == 0)
    def _():
        xb_ref[...] = x_ref[...].astype(BF16)
        os_ref[...] = _dot(xb_ref[...], ws_ref[...]) + bs_ref[...]

    o_ref[...] = _dot(xb_ref[...], w_ref[...]) + b_ref[...]


def _proj(x, w_bf, b, ws_bf, bs):
    T, K = x.shape
    N = w_bf.shape[1]
    tm = min(PROJ_TM, T)
    tn = PROJ_TN
    return pl.pallas_call(
        _proj_kernel,
        grid=(T // tm, N // tn),
        in_specs=[pl.BlockSpec((tm, K), lambda i, j: (i, 0)),
                  pl.BlockSpec((K, tn), lambda i, j: (0, j)),
                  pl.BlockSpec((1, tn), lambda i, j: (0, j)),
                  pl.BlockSpec((K, SMALL_W), lambda i, j: (0, 0)),
                  pl.BlockSpec((1, SMALL_W), lambda i, j: (0, 0))],
        out_specs=[pl.BlockSpec((tm, tn), lambda i, j: (i, j)),
                   pl.BlockSpec((tm, SMALL_W), lambda i, j: (i, 0))],
        out_shape=[jax.ShapeDtypeStruct((T, N), F32), jax.ShapeDtypeStruct((T, SMALL_W), F32)],
        scratch_shapes=[pltpu.VMEM((tm, K), BF16)],
        compiler_params=_params(("parallel", "arbitrary")),
    )(x, w_bf, b, ws_bf, bs)


def _mlstm_kernel(qk_ref, v_ref, og_ref, sm_ref, smt_ref, cw_ref, cb_ref, nw_ref, o_ref,
                  ext_ref, c_ref, n_ref, m_ref):
    L = M_CHUNK
    c = pl.program_id(1)

    @pl.when(c == 0)
    def _():
        ext_ref[0:SUBLANES, :] = jnp.zeros((SUBLANES, 2 * M_QKW), F32)
        c_ref[...] = jnp.zeros_like(c_ref)
        n_ref[...] = jnp.zeros_like(n_ref)
        m_ref[...] = jnp.zeros_like(m_ref)

    ext_ref[SUBLANES:, :] = qk_ref[...]
    acc = cb_ref[...] + cw_ref[CONV_W - 1:CONV_W, :] * ext_ref[SUBLANES:, :]
    for j in range(CONV_W - 1):
        off = SUBLANES - (CONV_W - 1) + j
        acc = acc + cw_ref[j:j + 1, :] * ext_ref[off:off + L, :]
    ext_ref[0:SUBLANES, :] = ext_ref[L:L + SUBLANES, :]
    qk = acc * _sigmoid(acc)

    row = lax.broadcasted_iota(jnp.int32, (L, L), 0)
    col = lax.broadcasted_iota(jnp.int32, (L, L), 1)
    causal = row >= col
    tri = jnp.where(causal, 1.0, 0.0).astype(BF16)
    triu = jnp.where(col >= row, 1.0, 0.0).astype(BF16)

    sm = sm_ref[...]
    smt = smt_ref[...]
    h1, h2, h3 = _split3(_log_sigmoid(sm))
    bcol_all = _dot(tri, h1) + _dot(tri, h2) + _dot(tri, h3)
    r1, r2, r3 = _split3(_log_sigmoid(smt))
    brow_all = _dot(r1, triu) + _dot(r2, triu) + _dot(r3, triu)

    for h in range(M_HEADS):
        q = qk[:, h * M_QK:(h + 1) * M_QK]
        k = qk[:, M_QKW + h * M_QK:M_QKW + (h + 1) * M_QK] * (M_QK ** -0.5)
        v = v_ref[:, h * M_V:(h + 1) * M_V]
        qb, kb, vb = q.astype(BF16), k.astype(BF16), v.astype(BF16)
        li_col = sm[:, h:h + 1]
        b_col = bcol_all[:, M_HEADS + h:M_HEADS + h + 1]
        b_end = b_col[L - 1:L, :]
        li_row = smt[h:h + 1, :]
        b_row = brow_all[M_HEADS + h:M_HEADS + h + 1, :]
        m_prev = m_ref[h:h + 1, 0:1]
        n_prev = n_ref[h:h + 1, :]
        c_prev = c_ref[h]

        d_log = jnp.where(causal, b_col + (li_row - b_row), -jnp.inf)
        m_inter = b_col + m_prev
        m_t = jnp.maximum(m_inter, jnp.max(d_log, axis=-1, keepdims=True))
        dec = jnp.exp(m_inter - m_t)
        s = _dot_nt(qb, kb) * jnp.exp(d_log - m_t)
        num = _dot(s.astype(BF16), vb) + dec * _dot(qb, c_prev.astype(BF16))
        den = jnp.sum(s, axis=-1, keepdims=True) + dec * jnp.sum(q * n_prev, axis=-1, keepdims=True)
        hh = num / jnp.maximum(jnp.abs(den), jnp.exp(-m_t))

        w_end = b_end - b_col + li_col
        g_end = jnp.max(w_end, axis=0, keepdims=True)
        ke = k * jnp.exp(w_end - g_end)
        d_c = _dot_tn(ke.astype(BF16), vb)
        d_n = jnp.sum(ke, axis=0, keepdims=True)
        m_new = jnp.maximum(b_end + m_prev, g_end)
        a = jnp.exp(b_end + m_prev - m_new)
        cc = jnp.exp(g_end - m_new)
        c_ref[h] = a * c_prev + cc * d_c
        n_ref[h:h + 1, :] = a * n_prev + cc * d_n
        m_ref[h:h + 1, :] = jnp.broadcast_to(m_new, (1, LANES))

        mu = jnp.mean(hh, axis=-1, keepdims=True)
        xc = hh - mu
        var = jnp.mean(xc * xc, axis=-1, keepdims=True)
        sl = slice(h * M_V, (h + 1) * M_V)
        y = xc * lax.rsqrt(var + NORM_EPS) * nw_ref[:, sl] * _sigmoid(og_ref[:, sl])
        o_ref[:, sl] = y.astype(o_ref.dtype)


def _mlstm(proj, small, small_t, conv_w, conv_b, norm_w, B, S):
    L = M_CHUNK
    NC = S // L
    T = B * S
    rowblk = lambda b, c: (b * NC + c, 0)
    const = lambda b, c: (0, 0)
    return pl.pallas_call(
        _mlstm_kernel,
        grid=(B, NC),
        in_specs=[pl.BlockSpec((L, 2 * M_QKW), rowblk),
                  pl.BlockSpec((L, M_VW), lambda b, c: (b * NC + c, 1)),
                  pl.BlockSpec((L, M_VW), lambda b, c: (b * NC + c, 2)),
                  pl.BlockSpec((L, SMALL_W), rowblk),
                  pl.BlockSpec((SUBLANES, L), lambda b, c: (0, b * NC + c)),
                  pl.BlockSpec((CONV_W, 2 * M_QKW), const),
                  pl.BlockSpec((1, 2 * M_QKW), const),
                  pl.BlockSpec((1, M_VW), const)],
        out_specs=pl.BlockSpec((L, M_VW), rowblk),
        out_shape=jax.ShapeDtypeStruct((T, M_VW), BF16),
        scratch_shapes=[pltpu.VMEM((L + SUBLANES, 2 * M_QKW), F32),
                        pltpu.VMEM((M_HEADS, M_QK, M_V), F32),
                        pltpu.VMEM((SUBLANES, M_QK), F32),
                        pltpu.VMEM((SUBLANES, LANES), F32)],
        compiler_params=_params(("parallel", "arbitrary")),
    )(proj, proj, proj, small, small_t, conv_w, conv_b, norm_w)


def _gla_kernel(qk_ref, v_ref, gg_ref, sm_ref, wg_ref, bg_ref, nw_ref, o_ref, st_ref):
    LC = G_CHUNK
    c = pl.program_id(1)

    @pl.when(c == 0)
    def _():
        st_ref[...] = jnp.zeros_like(st_ref)

    pre = _dot(sm_ref[...].astype(BF16), wg_ref[...]) + bg_ref[...]
    log_a = _log_sigmoid(pre) / G_TAU

    row = lax.broadcasted_iota(jnp.int32, (LC, LC), 0)
    col = lax.broadcasted_iota(jnp.int32, (LC, LC), 1)
    causal = row >= col
    tri = jnp.where(causal, 1.0, 0.0).astype(BF16)

    for j in range(G_STEP // LC):
        rs = slice(j * LC, (j + 1) * LC)
        a1, a2, a3 = _split3(log_a[rs, :])
        cum = _dot(tri, a1) + _dot(tri, a2) + _dot(tri, a3)
        cum_end = cum[LC - 1:LC, :]
        e_q = jnp.exp(cum)
        e_k = jnp.exp(-cum)
        e_end = jnp.exp(cum_end - cum)
        e_dec = jnp.exp(cum_end)
        for h in range(G_HEADS):
            ks = slice(h * G_K, (h + 1) * G_K)
            vs = slice(h * G_V, (h + 1) * G_V)
            q = qk_ref[rs, ks] * (G_K ** -0.5)
            k = qk_ref[rs, G_KW + h * G_K:G_KW + (h + 1) * G_K]
            vb = v_ref[rs, vs].astype(BF16)
            q_in = (q * e_q[:, ks]).astype(BF16)
            k_in = (k * e_k[:, ks]).astype(BF16)
            k_end = (k * e_end[:, ks]).astype(BF16)
            st = st_ref[h]
            s = jnp.where(causal, _dot_nt(q_in, k_in), 0.0)
            o = _dot(s.astype(BF16), vb) + _dot_nt(q_in, st.astype(BF16))
            st_ref[h] = st * e_dec[:, ks] + _dot_tn(vb, k_end)
            ms = jnp.mean(o * o, axis=-1, keepdims=True)
            g = gg_ref[rs, vs]
            y = o * lax.rsqrt(ms + NORM_EPS) * nw_ref[:, vs] * (g * _sigmoid(g))
            o_ref[rs, vs] = y.astype(o_ref.dtype)


def _gla(proj, small, wg_pad, bg, norm_w, B, S):
    L = G_STEP
    NC = S // L
    T = B * S
    rowblk = lambda b, c: (b * NC + c, 0)
    const = lambda b, c: (0, 0)
    return pl.pallas_call(
        _gla_kernel,
        grid=(B, NC),
        in_specs=[pl.BlockSpec((L, 2 * G_KW), lambda b, c: (b * NC + c, 3)),
                  pl.BlockSpec((L, G_VW), lambda b, c: (b * NC + c, 4)),
                  pl.BlockSpec((L, G_VW), lambda b, c: (b * NC + c, 5)),
                  pl.BlockSpec((L, SMALL_W), rowblk),
                  pl.BlockSpec((SMALL_W, G_KW), const),
                  pl.BlockSpec((1, G_KW), const),
                  pl.BlockSpec((1, G_VW), const)],
        out_specs=pl.BlockSpec((L, G_VW), rowblk),
        out_shape=jax.ShapeDtypeStruct((T, G_VW), BF16),
        scratch_shapes=[pltpu.VMEM((G_HEADS, G_V, G_K), F32)],
        compiler_params=_params(("parallel", "arbitrary")),
    )(proj, proj, proj, small, wg_pad, bg, norm_w)


def _layer_norm(r, g, b):
    mu = jnp.mean(r, axis=-1, keepdims=True)
    xc = r - mu
    var = jnp.mean(xc * xc, axis=-1, keepdims=True)
    return xc * lax.rsqrt(var + NORM_EPS) * g + b


def _mix_kernel(alpha, ym_ref, yg_ref, gm_ref, gg_ref, x_ref, wbm_ref, wbg_ref, wout_ref, lg_ref, lb_ref,
                wr_ref, br_ref, x1_ref, tope_ref, gate_ref, rank_ref, cnt_ref, carry_ref):
    tm = x_ref.shape[0]
    i = pl.program_id(0)

    @pl.when(i == 0)
    def _():
        carry_ref[...] = jnp.zeros_like(carry_ref)

    pm = _dot(ym_ref[...], wbm_ref[...])
    pg = _dot(yg_ref[...], wbg_ref[...])
    z = _sigmoid(gm_ref[...]) * pm + _sigmoid(gg_ref[...]) * pg
    mix = _dot(z.astype(BF16), wout_ref[...])
    x1 = _layer_norm(alpha * x_ref[...] + mix, lg_ref[...], lb_ref[...])
    x1_ref[...] = x1

    logits = _dot_nt(wr_ref[...], x1.astype(BF16)) + br_ref[...]
    eidx = lax.broadcasted_iota(jnp.int32, (N_EXPERTS, tm), 0)
    vals, hots = [], []
    cur = logits
    for k in range(TOP_K):
        mx = jnp.max(cur, axis=0, keepdims=True)
        idx = jnp.min(jnp.where(cur == mx, eidx, N_EXPERTS), axis=0, keepdims=True)
        hot = eidx == idx
        cur = jnp.where(hot, -jnp.inf, cur)
        vals.append(mx)
        hots.append(hot)
        tope_ref[k:k + 1, :] = idx
    exps = [jnp.exp(v - vals[0]) for v in vals]
    tot = exps[0] + exps[1] + exps[2] + exps[3]
    for k in range(TOP_K):
        gate_ref[k:k + 1, :] = exps[k] / tot

    sel = jnp.zeros((N_EXPERTS, tm), F32)
    for hot in hots:
        sel = sel + jnp.where(hot, 1.0, 0.0)
    row = lax.broadcasted_iota(jnp.int32, (tm, tm), 0)
    col = lax.broadcasted_iota(jnp.int32, (tm, tm), 1)
    before = jnp.where(row < col, 1.0, 0.0).astype(BF16)
    pos = _dot(sel.astype(BF16), before) + carry_ref[:, 0:1]
    for k in range(TOP_K):
        rk = jnp.sum(jnp.where(hots[k], pos, 0.0), axis=0, keepdims=True)
        rank_ref[k:k + 1, :] = rk.astype(jnp.int32)
    carry_ref[...] = carry_ref[...] + jnp.sum(sel, axis=1, keepdims=True)
    cnt_ref[...] = carry_ref[...]


def _mix(ym, yg, proj, x2d, wbm, wbg, wout, ln_g, ln_b, wr_t, br, alpha):
    T, D = x2d.shape
    tm = min(MIX_TM, T)
    rowblk = lambda i: (i, 0)
    const = lambda i: (0, 0)
    once = dict(pipeline_mode=pl.Buffered(1))
    return pl.pallas_call(
        functools.partial(_mix_kernel, alpha),
        grid=(T // tm,),
        in_specs=[pl.BlockSpec((tm, M_VW), rowblk),
                  pl.BlockSpec((tm, G_VW), rowblk),
                  pl.BlockSpec((tm, D), lambda i: (i, 3)),
                  pl.BlockSpec((tm, D), lambda i: (i, 4)),
                  pl.BlockSpec((tm, D), rowblk),
                  pl.BlockSpec((M_VW, D), const, **once),
                  pl.BlockSpec((G_VW, D), const, **once),
                  pl.BlockSpec((D, D), const, **once),
                  pl.BlockSpec((1, D), const),
                  pl.BlockSpec((1, D), const),
                  pl.BlockSpec((N_EXPERTS, D), const),
                  pl.BlockSpec((N_EXPERTS, 1), const)],
        out_specs=[pl.BlockSpec((tm, D), rowblk),
                   pl.BlockSpec((TOP_K, tm), lambda i: (0, i)),
                   pl.BlockSpec((TOP_K, tm), lambda i: (0, i)),
                   pl.BlockSpec((TOP_K, tm), lambda i: (0, i)),
                   pl.BlockSpec((N_EXPERTS, LANES), const)],
        out_shape=[jax.ShapeDtypeStruct((T, D), F32),
                   jax.ShapeDtypeStruct((TOP_K, T), jnp.int32),
                   jax.ShapeDtypeStruct((TOP_K, T), F32),
                   jax.ShapeDtypeStruct((TOP_K, T), jnp.int32),
                   jax.ShapeDtypeStruct((N_EXPERTS, LANES), F32)],
        scratch_shapes=[pltpu.VMEM((N_EXPERTS, LANES), F32)],
        compiler_params=_params(("arbitrary",)),
    )(ym, yg, proj, proj, x2d, wbm, wbg, wout, ln_g, ln_b, wr_t, br)


def _row_copy(src, src_row, dst, dst_row, sem):
    return pltpu.make_async_copy(src.at[pl.ds(src_row, 1)], dst.at[pl.ds(dst_row, 1)], sem)


def _dispatch_kernel(n_tok, dest_ref, x1_hbm, xs_hbm, buf_ref, load_sems, row_sems):
    tm = buf_ref.shape[1]
    i = pl.program_id(0)
    n = pl.num_programs(0)

    def load(tile):
        slot = tile % 3
        return pltpu.make_async_copy(x1_hbm.at[pl.ds(pl.multiple_of(tile * tm, tm), tm)], buf_ref.at[slot],
                                     load_sems.at[slot])

    def wait_rows(tile):
        for k in range(TOP_K):
            pltpu.make_async_copy(buf_ref.at[tile % 3], xs_hbm.at[pl.ds(0, tm)], row_sems.at[tile % 2]).wait()

    @pl.when(i == 0)
    def _():
        load(0).start()

    @pl.when(i + 1 < n)
    def _():
        load(i + 1).start()

    load(i).wait()
    src = buf_ref.at[i % 3]
    base = i * tm

    def issue(t, carry):
        for k in range(TOP_K):
            _row_copy(src, t, xs_hbm, dest_ref[k * n_tok + base + t], row_sems.at[i % 2]).start()
        return carry

    lax.fori_loop(0, tm, issue, 0, unroll=ROW_DMA_UNROLL)

    @pl.when(i > 0)
    def _():
        wait_rows(i - 1)

    @pl.when(i == n - 1)
    def _():
        wait_rows(i)


def _dispatch(dest_flat, x1, n_rows):
    T, D = x1.shape
    tm = min(DISP_TM, T)
    return pl.pallas_call(
        functools.partial(_dispatch_kernel, T),
        grid_spec=pltpu.PrefetchScalarGridSpec(
            num_scalar_prefetch=1,
            grid=(T // tm,),
            in_specs=[pl.BlockSpec(memory_space=pl.ANY)],
            out_specs=pl.BlockSpec(memory_space=pl.ANY),
            scratch_shapes=[pltpu.VMEM((3, tm, D), x1.dtype),
                            pltpu.SemaphoreType.DMA((3,)),
                            pltpu.SemaphoreType.DMA((2,))]),
        out_shape=jax.ShapeDtypeStruct((n_rows, D), x1.dtype),
        compiler_params=pltpu.CompilerParams(dimension_semantics=("arbitrary",), has_side_effects=True),
    )(dest_flat, x1)


def _zero_rows_kernel(start_ref, cnt_ref, xs_in_hbm, xs_hbm, zrow_ref, sem):
    del xs_in_hbm
    zrow_ref[...] = jnp.zeros_like(zrow_ref)

    def per_range(e, carry):
        start, cnt = start_ref[e], cnt_ref[e]
        head = jnp.minimum(cnt, (-start) & (SUBLANES - 1))
        mid = start + head
        n_big = (cnt - head) // ZERO_CHUNK
        mid2 = mid + n_big * ZERO_CHUNK
        n_small = (cnt - head - n_big * ZERO_CHUNK) // SUBLANES
        tail = mid2 + n_small * SUBLANES
        n_tail = cnt - head - n_big * ZERO_CHUNK - n_small * SUBLANES

        def blocks(first, n, size):
            def copy(r):
                dst = pl.multiple_of(first + r * size, SUBLANES)
                return pltpu.make_async_copy(zrow_ref.at[0:size], xs_hbm.at[pl.ds(dst, size)], sem)

            def issue(r, c):
                copy(r).start()
                return c

            def drain(r, c):
                copy(0).wait()
                return c

            lax.fori_loop(0, n, issue, 0)
            lax.fori_loop(0, n, drain, 0)

        def rows(first, n):
            def issue(r, c):
                _row_copy(zrow_ref, 0, xs_hbm, first + r, sem).start()
                return c

            def drain(r, c):
                _row_copy(zrow_ref, 0, xs_hbm, 0, sem).wait()
                return c

            lax.fori_loop(0, n, issue, 0)
            lax.fori_loop(0, n, drain, 0)

        rows(start, head)
        blocks(mid, n_big, ZERO_CHUNK)
        blocks(mid2, n_small, SUBLANES)
        rows(tail, n_tail)
        return carry

    lax.fori_loop(0, start_ref.shape[0], per_range, 0)


def _zero_rows(start, cnt, xs):
    return pl.pallas_call(
        _zero_rows_kernel,
        grid_spec=pltpu.PrefetchScalarGridSpec(
            num_scalar_prefetch=2,
            grid=(1,),
            in_specs=[pl.BlockSpec(memory_space=pl.ANY)],
            out_specs=pl.BlockSpec(memory_space=pl.ANY),
            scratch_shapes=[pltpu.VMEM((ZERO_CHUNK, xs.shape[1]), xs.dtype), pltpu.SemaphoreType.DMA(())]),
        out_shape=jax.ShapeDtypeStruct(xs.shape, xs.dtype),
        input_output_aliases={2: 0},
        compiler_params=pltpu.CompilerParams(dimension_semantics=("arbitrary",), has_side_effects=True),
    )(start, cnt, xs)


def _expert_kernel(te_ref, tr_ref, na_ref, xs_ref, wgu_hbm, bgu_ref, wd_hbm, bd_ref, o_ref,
                   wgu_ref, wd_ref, stage_ref, sems):
    i = pl.program_id(0)
    fh, d = wd_ref.shape
    active = i < na_ref[0]
    e = te_ref[i]
    changed = jnp.logical_or(i == 0, e != te_ref[jnp.maximum(i - 1, 0)])

    def swiglu(a, tf, f):
        glu = slice(f * tf, (f + 1) * tf)
        lin = slice(fh + f * tf, fh + (f + 1) * tf)
        g = _dot(a, wgu_ref[:, glu]) + bgu_ref[:, glu]
        l = _dot(a, wgu_ref[:, lin]) + bgu_ref[:, lin]
        a_glu = jnp.minimum(g, SWIGLU_LIMIT)
        a_lin = jnp.clip(l, -SWIGLU_LIMIT, SWIGLU_LIMIT)
        return (a_glu * _sigmoid(SWIGLU_ALPHA * a_glu) * (a_lin + 1.0)).astype(BF16)

    @pl.when(jnp.logical_and(active, changed))
    def _():
        ts = stage_ref.shape[2]
        nf = fh // ts
        n_piece = 3 * nf

        def copies(p):
            b = p % 2
            if p < 2 * nf:
                f, lin = divmod(p, 2)
                c0 = lin * fh + f * ts
                return [pltpu.make_async_copy(wgu_hbm.at[e, :, pl.ds(c0, ts)], stage_ref.at[b], sems.at[b])]
            r0 = (p - 2 * nf) * ts
            return [pltpu.make_async_copy(wd_hbm.at[e, pl.ds(r0, ts), pl.ds(j * ts, ts)],
                                          stage_ref.at[b, pl.ds(j * ts, ts), :], sems.at[b])
                    for j in range(d // ts)]

        def land(p):
            for cp in copies(p):
                cp.wait()
            b = p % 2
            if p < 2 * nf:
                f, lin = divmod(p, 2)
                c0 = lin * fh + f * ts
                wgu_ref[:, c0:c0 + ts] = stage_ref[b].astype(BF16)
            else:
                r0 = (p - 2 * nf) * ts
                for j in range(d // ts):
                    wd_ref[r0:r0 + ts, j * ts:(j + 1) * ts] = stage_ref[b, j * ts:(j + 1) * ts, :].astype(BF16)

        for p in range(2):
            for cp in copies(p):
                cp.start()
        a = xs_ref[...].astype(BF16)
        acts, acc = [], None
        for p in range(n_piece):
            land(p)
            if p + 2 < n_piece:
                for cp in copies(p + 2):
                    cp.start()
            if p < 2 * nf and p % 2 == 1:
                acts.append(swiglu(a, ts, p // 2))
            if p >= 2 * nf:
                f = p - 2 * nf
                part = _dot(acts[f], wd_ref[f * ts:(f + 1) * ts, :])
                acc = part if acc is None else acc + part
        o_ref[...] = acc + bd_ref[...]

    def ffn(a):
        tf = min(EXP_TF, fh)
        acc = None
        for f in range(fh // tf):
            part = _dot(swiglu(a, tf, f), wd_ref[f * tf:(f + 1) * tf, :])
            acc = part if acc is None else acc + part
        return acc + bd_ref[...]

    half = xs_ref.shape[0] // 2
    later = jnp.logical_and(active, jnp.logical_not(changed))
    short = tr_ref[i] <= half

    @pl.when(jnp.logical_and(later, jnp.logical_not(short)))
    def _():
        o_ref[...] = ffn(xs_ref[...].astype(BF16))

    @pl.when(jnp.logical_and(later, short))
    def _():
        o_ref[0:half, :] = ffn(xs_ref[0:half, :].astype(BF16))
        o_ref[half:, :] = jnp.zeros((half, d), F32)

    @pl.when(i >= na_ref[0])
    def _():
        o_ref[...] = jnp.zeros_like(o_ref)


def _experts(tile_e, tile_rows, n_active, xs, w_gu, b_gu, w_down, b_down):
    P, D = xs.shape
    E, _, F2 = w_gu.shape
    Fh = F2 // 2
    n_tiles = P // EXP_TM

    def tile(i, na):
        return jnp.minimum(i, na[0] - 1)

    return pl.pallas_call(
        _expert_kernel,
        grid_spec=pltpu.PrefetchScalarGridSpec(
            num_scalar_prefetch=3,
            grid=(n_tiles,),
            in_specs=[
                pl.BlockSpec((EXP_TM, D), lambda i, te, tr, na: (tile(i, na), 0)),
                pl.BlockSpec(memory_space=pl.ANY),
                pl.BlockSpec((None, 1, F2), lambda i, te, tr, na: (te[tile(i, na)], 0, 0)),
                pl.BlockSpec(memory_space=pl.ANY),
                pl.BlockSpec((None, 1, D), lambda i, te, tr, na: (te[tile(i, na)], 0, 0)),
            ],
            out_specs=pl.BlockSpec((EXP_TM, D), lambda i, te, tr, na: (i, 0)),
            scratch_shapes=[pltpu.VMEM((D, F2), BF16),
                            pltpu.VMEM((Fh, D), BF16),
                            pltpu.VMEM((2, D, EXP_WPIECE), F32),
                            pltpu.SemaphoreType.DMA((2,))]),
        out_shape=jax.ShapeDtypeStruct((P, D), F32),
        compiler_params=_params(("arbitrary",), EXP_VMEM_LIMIT),
    )(tile_e, tile_rows, n_active, xs, w_gu, b_gu, w_down, b_down)


def _combine_kernel(alpha, n_tok, dest_ref, ys_hbm, x1_ref, gate_ref, lg_ref, lb_ref, o_ref, rows_ref, sems):
    tm = x1_ref.shape[0]
    i = pl.program_id(0)
    last = pl.num_programs(0) - 1
    buf = i % 2

    def start_row(tile, b, t):
        for k in range(TOP_K):
            _row_copy(ys_hbm, dest_ref[k * n_tok + tile * tm + t], rows_ref.at[b, k], t, sems.at[b]).start()

    def wait_rows(b):
        for k in range(TOP_K):
            pltpu.make_async_copy(ys_hbm.at[pl.ds(0, tm)], rows_ref.at[b, k], sems.at[b]).wait()

    @pl.when(i == 0)
    def _():
        def issue(t, carry):
            start_row(0, 0, t)
            return carry

        lax.fori_loop(0, tm, issue, 0, unroll=ROW_DMA_UNROLL)

    wait_rows(buf)

    nxt = jnp.minimum(i + 1, last)
    grp = tm // COMB_GROUPS
    for j in range(COMB_GROUPS):
        for t in range(j * grp, (j + 1) * grp):
            start_row(nxt, 1 - buf, t)
        rs = slice(j * grp, (j + 1) * grp)
        ff = gate_ref[rs, 0:1] * rows_ref[buf, 0, rs, :]
        for k in range(1, TOP_K):
            ff = ff + gate_ref[rs, k:k + 1] * rows_ref[buf, k, rs, :]
        o_ref[rs, :] = _layer_norm(alpha * x1_ref[rs, :] + ff, lg_ref[...], lb_ref[...])

    @pl.when(i == last)
    def _():
        wait_rows(1 - buf)


def _combine(dest_flat, ys, x1, gates_col, ln_g, ln_b, alpha):
    T, D = x1.shape
    tm = min(COMB_TM, T)
    return pl.pallas_call(
        functools.partial(_combine_kernel, alpha, T),
        grid_spec=pltpu.PrefetchScalarGridSpec(
            num_scalar_prefetch=1,
            grid=(T // tm,),
            in_specs=[pl.BlockSpec(memory_space=pl.ANY),
                      pl.BlockSpec((tm, D), lambda i, d: (i, 0)),
                      pl.BlockSpec((tm, TOP_K), lambda i, d: (i, 0)),
                      pl.BlockSpec((1, D), lambda i, d: (0, 0)),
                      pl.BlockSpec((1, D), lambda i, d: (0, 0))],
            out_specs=pl.BlockSpec((tm, D), lambda i, d: (i, 0)),
            scratch_shapes=[pltpu.VMEM((2, TOP_K, tm, D), F32), pltpu.SemaphoreType.DMA((2,))]),
        out_shape=jax.ShapeDtypeStruct((T, D), F32),
        compiler_params=_params(("arbitrary",)),
    )(dest_flat, ys, x1, gates_col, ln_g, ln_b)


def _layer(x2d, B, S, alpha, w_in, b_in, conv_w, conv_b, w_gla_gate, b_gla_gate, m_norm_w, g_norm_w,
           w_branch_m, w_branch_g, w_out, ln1_g, ln1_b, w_router, b_router, w_gu, b_gu, w_down, b_down,
           ln2_g, ln2_b):
    T, D = x2d.shape
    m_end = 2 * M_QKW + 2 * M_VW
    g_beg = m_end + 2 * M_HEADS
    g_end = g_beg + 2 * G_KW + 2 * G_VW
    mg_beg = g_end + G_RANK
    main_cols = [(0, m_end), (g_beg, g_end), (mg_beg, mg_beg + 2 * D)]
    w_main = jnp.concatenate([w_in[:, a:b] for a, b in main_cols], axis=1).astype(BF16)
    b_main = jnp.concatenate([b_in[a:b] for a, b in main_cols])[None, :]
    n_small = 2 * M_HEADS + G_RANK
    w_small = jnp.concatenate([w_in[:, m_end:g_beg], w_in[:, g_end:mg_beg],
                               jnp.zeros((D, SMALL_W - n_small), F32)], axis=1).astype(BF16)
    b_small = jnp.concatenate([b_in[m_end:g_beg], b_in[g_end:mg_beg], jnp.zeros((SMALL_W - n_small,), F32)])[None, :]

    proj, small = _proj(x2d, w_main, b_main, w_small, b_small)
    small_t = small[:, :SUBLANES].T

    ym = _mlstm(proj, small, small_t, conv_w, conv_b[None, :], m_norm_w[None, :], B, S)
    wg_pad = jnp.zeros((SMALL_W, G_KW), F32).at[2 * M_HEADS:n_small].set(w_gla_gate).astype(BF16)
    yg = _gla(proj, small, wg_pad, b_gla_gate[None, :], g_norm_w[None, :], B, S)

    x1, top_e, gates, rank, cnt = _mix(
        ym, yg, proj, x2d, w_branch_m.astype(BF16), w_branch_g.astype(BF16), w_out.astype(BF16),
        ln1_g[None, :], ln1_b[None, :], w_router.T.astype(BF16), b_router[:, None], alpha)

    counts = cnt[:, 0].astype(jnp.int32)
    padded = (counts + EXP_TM - 1) // EXP_TM * EXP_TM
    pend = jnp.cumsum(padded)
    pstart = pend - padded
    n_tiles = (T * TOP_K) // EXP_TM + N_EXPERTS
    tile_start = jnp.arange(n_tiles, dtype=jnp.int32) * EXP_TM
    tile_e = jnp.sum((pend[None, :] <= tile_start[:, None]).astype(jnp.int32), axis=1)
    tile_e = jnp.minimum(tile_e, N_EXPERTS - 1)
    tile_rows = jnp.clip(counts[tile_e] - (tile_start - pstart[tile_e]), 0, EXP_TM)
    n_active = (pend[-1:] // EXP_TM).astype(jnp.int32)
    dest = rank
    for e in range(N_EXPERTS):
        dest = dest + jnp.where(top_e == e, pstart[e], 0)
    dest = dest.reshape(-1)

    xs = _dispatch(dest, x1, n_tiles * EXP_TM)
    n_rows = n_tiles * EXP_TM
    hole_start = jnp.concatenate([pstart + counts, pend[-1:]])
    hole_cnt = jnp.concatenate([padded - counts, n_rows - pend[-1:]])
    xs = _zero_rows(hole_start, hole_cnt, xs)
    ys = _experts(tile_e, tile_rows, n_active, xs, w_gu, b_gu[:, None, :], w_down, b_down[:, None, :])
    return _combine(dest, ys, x1, gates.T, ln2_g[None, :], ln2_b[None, :], alpha)


def kernel(x, w_in, b_in, conv_w, conv_b, w_gla_gate, b_gla_gate, m_norm_w, g_norm_w, w_branch_m, w_branch_g, w_out, ln1_g, ln1_b, w_router, b_router, w_gu, b_gu, w_down, b_down, ln2_g, ln2_b):
    B, S, D = x.shape
    depth = w_in.shape[0]
    alpha = (2 * depth) ** 0.25
    x2d = x.reshape(B * S, D)
    for l in range(depth):
        x2d = _layer(x2d, B, S, alpha, w_in[l], b_in[l], conv_w[l], conv_b[l], w_gla_gate[l], b_gla_gate[l],
                     m_norm_w[l], g_norm_w[l], w_branch_m[l], w_branch_g[l], w_out[l], ln1_g[l], ln1_b[l],
                     w_router[l], b_router[l], w_gu[l], b_gu[l], w_down[l], b_down[l], ln2_g[l], ln2_b[l])
    return x2d.reshape(B, S, D)
```
